```python
import math
import jax, jax.numpy as jnp
from jax import lax
import numpy as np

D_MODEL = 2048
BATCH = 4
SEQ = 4096
DEPTH = 1

MIX_WIDTH = D_MODEL
HEAD_DIM = 128
ATTN_WIDTH = MIX_WIDTH // 2
ATTN_HEADS = ATTN_WIDTH // HEAD_DIM
KV_HEADS = 2
GQA_GROUP = ATTN_HEADS // KV_HEADS
WINDOW = 128
ATTN_BLOCK = 128
ROPE_THETA = 10000.0

GLA_WIDTH = MIX_WIDTH - ATTN_WIDTH
GLA_HEADS = 4
GLA_KEY_WIDTH = GLA_WIDTH // 2
GLA_DK = GLA_KEY_WIDTH // GLA_HEADS
GLA_DV = GLA_WIDTH // GLA_HEADS
GLA_DECAY_RANK = 16
GLA_GATE_NORMALIZER = 16.0
GLA_CHUNK = 64

D_FF = 5632
CONV_WIDTH = 3
NORM_EPS = 1e-6

IN_WIDTHS = (ATTN_WIDTH,
             KV_HEADS * HEAD_DIM,
             KV_HEADS * HEAD_DIM,
             GLA_KEY_WIDTH,
             GLA_KEY_WIDTH,
             GLA_WIDTH,
             GLA_WIDTH,
             GLA_DECAY_RANK,
             GLA_DECAY_RANK)
IN_TOTAL = sum(IN_WIDTHS)

kernel_name = "hymba_style_swa_gla_convffn_encoder"


def rms_norm(x, g):
    xf = x.astype(jnp.float32)
    y = xf * lax.rsqrt(jnp.mean(xf * xf, axis=-1, keepdims=True) + NORM_EPS)
    return (y * g.astype(jnp.float32)).astype(x.dtype)


def rope(x, pos):
    half = x.shape[-1] // 2
    inv = 1.0 / (ROPE_THETA ** (jnp.arange(half, dtype=jnp.float32) / half))
    ang = pos.astype(jnp.float32)[:, None] * inv[None, :]
    cos = jnp.cos(ang)[:, None, :]
    sin = jnp.sin(ang)[:, None, :]
    xf = x.astype(jnp.float32)
    x1, x2 = xf[..., :half], xf[..., half:]
    return jnp.concatenate([x1 * cos - x2 * sin, x2 * cos + x1 * sin], axis=-1).astype(x.dtype)


def window_attention(q, k, v, sink):
    B, T, HQ, D = q.shape
    nb = T // ATTN_BLOCK
    qb = q.reshape(B, nb, ATTN_BLOCK, KV_HEADS, GQA_GROUP, D)
    pad = ((0, 0), (ATTN_BLOCK, ATTN_BLOCK), (0, 0), (0, 0))
    kb = jnp.pad(k, pad).reshape(B, nb + 2, ATTN_BLOCK, KV_HEADS, D)
    vb = jnp.pad(v, pad).reshape(B, nb + 2, ATTN_BLOCK, KV_HEADS, D)
    kw = jnp.concatenate([kb[:, :-2], kb[:, 1:-1], kb[:, 2:]], axis=2)
    vw = jnp.concatenate([vb[:, :-2], vb[:, 1:-1], vb[:, 2:]], axis=2)
    s = jnp.einsum('bnqhgd,bnshd->bhgnqs', qb, kw).astype(jnp.float32) * (D ** -0.5)
    qi = jnp.arange(ATTN_BLOCK)[:, None]
    sj = jnp.arange(3 * ATTN_BLOCK)[None, :]
    rel = sj - ATTN_BLOCK - qi
    kpos = jnp.arange(nb)[:, None, None] * ATTN_BLOCK - ATTN_BLOCK + sj[None]
    mask = (jnp.abs(rel) <= WINDOW)[None] & (kpos >= 0) & (kpos < T)
    s = jnp.where(mask, s, -jnp.inf)
    sk = sink.astype(jnp.float32).reshape(KV_HEADS, GQA_GROUP)[None, :, :, None, None, None]
    m = jnp.maximum(jnp.max(s, axis=-1, keepdims=True), sk)
    p = jnp.exp(s - m)
    p = p / (jnp.sum(p, axis=-1, keepdims=True) + jnp.exp(sk - m))
    o = jnp.einsum('bhgnqs,bnshd->bnqhgd', p.astype(v.dtype), vw)
    return o.reshape(B, T, HQ * D)


def gla_chunked(q, k, v, g, strict):
    B, H, T, DK = q.shape
    DV = v.shape[-1]
    C = GLA_CHUNK
    n = T // C
    q = q.reshape(B, H, n, C, DK)
    k = k.reshape(B, H, n, C, DK)
    g = g.reshape(B, H, n, C, DK)
    v = v.reshape(B, H, n, C, DV)
    b = jnp.cumsum(g, axis=3)
    b_last = b[:, :, :, -1:, :]
    b_ref = b[:, :, :, C // 2:C // 2 + 1, :]
    a = jnp.einsum('bhnid,bhnjd->bhnij', q * jnp.exp(b - b_ref), k * jnp.exp(b_ref - b))
    mask = jnp.tril(jnp.ones((C, C), dtype=bool), -1 if strict else 0)
    a = jnp.where(mask, a, 0.0)
    o_intra = jnp.einsum('bhnij,bhnje->bhnie', a, v)
    q_in = q * jnp.exp(b)
    k_out = k * jnp.exp(b_last - b)
    chunk_decay = jnp.exp(b_last[:, :, :, 0, :])

    def step(state, inp):
        qc, kc, vc, dc = inp
        o = jnp.einsum('bhid,bhde->bhie', qc, state)
        state = state * dc[..., None] + jnp.einsum('bhjd,bhje->bhde', kc, vc)
        return state, o

    s0 = jnp.zeros((B, H, DK, DV), jnp.float32)
    xs = (jnp.moveaxis(q_in, 2, 0), jnp.moveaxis(k_out, 2, 0),
          jnp.moveaxis(v, 2, 0), jnp.moveaxis(chunk_decay, 2, 0))
    _, o_inter = lax.scan(step, s0, xs)
    o_inter = jnp.moveaxis(o_inter, 0, 2)
    return (o_intra + o_inter).reshape(B, H, T, DV)


def bidirectional_gla(q, k, v, lr_f, lr_b, wa2_f, ba_f, wa2_b, ba_b, gate, out_norm_g):
    B, T, _ = q.shape
    f32 = jnp.float32

    def heads(t, d):
        return jnp.transpose(t.astype(f32).reshape(B, T, GLA_HEADS, d), (0, 2, 1, 3))

    qh = heads(q, GLA_DK) * (GLA_DK ** -0.5)
    kh = heads(k, GLA_DK)
    vh = heads(v, GLA_DV)
    g_f = jax.nn.log_sigmoid((lr_f @ wa2_f + ba_f).astype(f32)) / GLA_GATE_NORMALIZER
    g_b = jax.nn.log_sigmoid((lr_b @ wa2_b + ba_b).astype(f32)) / GLA_GATE_NORMALIZER
    gf = heads(g_f, GLA_DK)
    gb = heads(g_b, GLA_DK)
    o_fwd = gla_chunked(qh, kh, vh, gf, strict=False)
    flip = lambda t: jnp.flip(t, axis=2)
    o_bwd = flip(gla_chunked(flip(qh), flip(kh), flip(vh), flip(gb), strict=True))
    o = jnp.transpose(o_fwd + o_bwd, (0, 2, 1, 3)).astype(v.dtype)
    o = rms_norm(o, out_norm_g)
    o = o * jax.nn.silu(gate.reshape(B, T, GLA_HEADS, GLA_DV))
    return o.reshape(B, T, GLA_WIDTH)


def conv_gated_ffn(h, w_up, conv_w, conv_b, w_down):
    u = h @ w_up
    up_ = jnp.pad(u, ((0, 0), (1, 1), (0, 0)))
    u = up_[:, :-2] * conv_w[0] + up_[:, 1:-1] * conv_w[1] + up_[:, 2:] * conv_w[2] + conv_b
    gate, val = jnp.split(u, 2, axis=-1)
    return (jax.nn.silu(gate) * val) @ w_down


def hybrid_layer(x, norm1_g, w_in, q_norm_g, k_norm_g, sink, wa2_f, ba_f, wa2_b, ba_b,
                 gla_out_norm_g, w_out, norm2_g, w_up, conv_w, conv_b, w_down):
    B, T, _ = x.shape
    pos = jnp.arange(T, dtype=jnp.int32)
    h = rms_norm(x, norm1_g)
    proj = h @ w_in
    offsets = []
    acc = 0
    for w in IN_WIDTHS[:-1]:
        acc += w
        offsets.append(acc)
    q_a, k_a, v_a, q_g, k_g, v_g, gate_g, lr_f, lr_b = jnp.split(proj, offsets, axis=-1)
    qa = rope(rms_norm(q_a.reshape(B, T, ATTN_HEADS, HEAD_DIM), q_norm_g), pos)
    ka = rope(rms_norm(k_a.reshape(B, T, KV_HEADS, HEAD_DIM), k_norm_g), pos)
    va = v_a.reshape(B, T, KV_HEADS, HEAD_DIM)
    o_attn = window_attention(qa, ka, va, sink)
    o_gla = bidirectional_gla(q_g, k_g, v_g, lr_f, lr_b, wa2_f, ba_f, wa2_b, ba_b,
                              gate_g, gla_out_norm_g)
    x = x + jnp.concatenate([o_attn, o_gla], axis=-1) @ w_out
    x = x + conv_gated_ffn(rms_norm(x, norm2_g), w_up, conv_w, conv_b, w_down)
    return x


def setup_inputs(seed: int = 0) -> dict:
    key = jax.random.key(seed)
    ks = jax.random.split(key, 20)
    f32 = jnp.float32
    L = DEPTH

    def nrm(k, shape, scale):
        return jax.random.normal(k, shape, f32) * scale

    return {
        "x": nrm(ks[0], (BATCH, SEQ, D_MODEL), 1.0),
        "norm1_g": 1.0 + nrm(ks[1], (L, D_MODEL), 0.02),
        "w_in": nrm(ks[2], (L, D_MODEL, IN_TOTAL), D_MODEL ** -0.5),
        "attn_q_norm_g": 1.0 + nrm(ks[3], (L, HEAD_DIM), 0.02),
        "attn_k_norm_g": 1.0 + nrm(ks[4], (L, HEAD_DIM), 0.02),
        "attn_sink": nrm(ks[5], (L, ATTN_HEADS), 0.5),
        "gla_wa2_fwd": nrm(ks[6], (L, GLA_DECAY_RANK, GLA_KEY_WIDTH), GLA_DECAY_RANK ** -0.5),
        "gla_ba_fwd": nrm(ks[7], (L, GLA_KEY_WIDTH), 0.1),
        "gla_wa2_bwd": nrm(ks[8], (L, GLA_DECAY_RANK, GLA_KEY_WIDTH), GLA_DECAY_RANK ** -0.5),
        "gla_ba_bwd": nrm(ks[9], (L, GLA_KEY_WIDTH), 0.1),
        "gla_out_norm_g": 1.0 + nrm(ks[10], (L, GLA_DV), 0.02),
        "w_out": nrm(ks[11], (L, MIX_WIDTH, D_MODEL), MIX_WIDTH ** -0.5),
        "norm2_g": 1.0 + nrm(ks[12], (L, D_MODEL), 0.02),
        "w_up": nrm(ks[13], (L, D_MODEL, 2 * D_FF), D_MODEL ** -0.5),
        "conv_w": nrm(ks[14], (L, CONV_WIDTH, 2 * D_FF), CONV_WIDTH ** -0.5),
        "conv_b": nrm(ks[15], (L, 2 * D_FF), 0.02),
        "w_down": nrm(ks[16], (L, D_FF, D_MODEL), D_FF ** -0.5),
    }


def reference(x, norm1_g, w_in, attn_q_norm_g, attn_k_norm_g, attn_sink, gla_wa2_fwd,
              gla_ba_fwd, gla_wa2_bwd, gla_ba_bwd, gla_out_norm_g, w_out, norm2_g,
              w_up, conv_w, conv_b, w_down):
    for l in range(DEPTH):
        x = hybrid_layer(x, norm1_g[l], w_in[l], attn_q_norm_g[l], attn_k_norm_g[l],
                         attn_sink[l], gla_wa2_fwd[l], gla_ba_fwd[l], gla_wa2_bwd[l],
                         gla_ba_bwd[l], gla_out_norm_g[l], w_out[l], norm2_g[l],
                         w_up[l], conv_w[l], conv_b[l], w_down[l])
    return x
```

```python
import functools

import jax
import jax.numpy as jnp
from jax import lax
from jax.experimental import pallas as pl
from jax.experimental.pallas import tpu as pltpu

F32 = jnp.float32
BF16 = jnp.bfloat16

HEAD_DIM = 128
KV_HEADS = 2
GQA_GROUP = 4
ATTN_HEADS = KV_HEADS * GQA_GROUP
ATTN_WIDTH = ATTN_HEADS * HEAD_DIM
WINDOW = 128
ROPE_THETA = 10000.0
GLA_HEADS = 4
GLA_DK = 128
GLA_DV = 256
GLA_KEY_WIDTH = GLA_HEADS * GLA_DK
GLA_WIDTH = GLA_HEADS * GLA_DV
GLA_DECAY_RANK = 16
GLA_GATE_NORMALIZER = 16.0
GLA_CHUNK = 64
NORM_EPS = 1e-6

OFF_QA = 0
OFF_KA = OFF_QA + ATTN_WIDTH
OFF_VA = OFF_KA + KV_HEADS * HEAD_DIM
OFF_QG = OFF_VA + KV_HEADS * HEAD_DIM
OFF_KG = OFF_QG + GLA_KEY_WIDTH
OFF_VG = OFF_KG + GLA_KEY_WIDTH
OFF_GG = OFF_VG + GLA_WIDTH
OFF_LR = OFF_GG + GLA_WIDTH
MAIN_WIDTH = OFF_LR
LR_PAD = 128

VMEM_LIMIT_BYTES = 56 * 1024 * 1024
INPROJ_TM = 512
INPROJ_TN = 1536
ATTN_TQ = 256
ROPE_ROWS = 512
OUTPROJ_TM = 512
FFN_TM = 512
FFN_TF = 512
HALO = 16


def _cparams(*sem):
    return pltpu.CompilerParams(dimension_semantics=sem, vmem_limit_bytes=VMEM_LIMIT_BYTES)


def _inproj_kernel(x_ref, g_ref, w_ref, wlr_ref, o_ref, lr_ref, h_ref):
    @pl.when(pl.program_id(1) == 0)
    def _():
        x = x_ref[...]
        ms = jnp.mean(x * x, axis=-1, keepdims=True)
        h = (x * lax.rsqrt(ms + NORM_EPS) * g_ref[...]).astype(BF16)
        h_ref[...] = h
        lr_ref[...] = jnp.dot(h, wlr_ref[...], preferred_element_type=F32)

    o_ref[...] = jnp.dot(h_ref[...], w_ref[...], preferred_element_type=F32).astype(o_ref.dtype)


def _inproj(x2, g1, w_main, w_lr):
    m, d = x2.shape
    tm, tn = INPROJ_TM, INPROJ_TN
    return pl.pallas_call(
        _inproj_kernel,
        grid=(m // tm, MAIN_WIDTH // tn),
        in_specs=[
            pl.BlockSpec((tm, d), lambda i, j: (i, 0)),
            pl.BlockSpec((1, d), lambda i, j: (0, 0)),
            pl.BlockSpec((d, tn), lambda i, j: (0, j)),
            pl.BlockSpec((d, LR_PAD), lambda i, j: (0, 0)),
        ],
        out_specs=[
            pl.BlockSpec((tm, tn), lambda i, j: (i, j)),
            pl.BlockSpec((tm, LR_PAD), lambda i, j: (i, 0)),
        ],
        out_shape=[
            jax.ShapeDtypeStruct((m, MAIN_WIDTH), BF16),
            jax.ShapeDtypeStruct((m, LR_PAD), F32),
        ],
        scratch_shapes=[pltpu.VMEM((tm, d), BF16)],
        compiler_params=_cparams("parallel", "arbitrary"),
        name="inproj",
    )(x2, g1, w_main, w_lr)


def _norm_rope(t, gain, cos, sin):
    t = t * lax.rsqrt(jnp.mean(t * t, axis=-1, keepdims=True) + NORM_EPS) * gain
    return t * cos + pltpu.roll(t, HEAD_DIM // 2, 1) * sin


def _attn_kernel(sink_ref, q_ref, k_ref, v_ref, cos_ref, sin_ref, qg_ref, kg_ref, o_ref, kr_ref,
                 *, tq, seq):
    i = pl.program_id(1)

    @pl.when(i == 0)
    def _():
        def body(c, carry):
            r0 = pl.multiple_of(c * ROPE_ROWS, ROPE_ROWS)
            cs = cos_ref[pl.ds(r0, ROPE_ROWS), :]
            sn = sin_ref[pl.ds(r0, ROPE_ROWS), :]
            for h in range(KV_HEADS):
                cols = slice(h * HEAD_DIM, (h + 1) * HEAD_DIM)
                kk = k_ref[0, pl.ds(r0, ROPE_ROWS), cols].astype(F32)
                kr_ref[pl.ds(r0, ROPE_ROWS), cols] = _norm_rope(kk, kg_ref[...], cs, sn).astype(BF16)
            return carry

        lax.fori_loop(0, seq // ROPE_ROWS, body, 0)

    win = tq + 2 * WINDOW
    q0 = pl.multiple_of(i * tq, tq)
    start = pl.multiple_of(jnp.clip(q0 - WINDOW, 0, seq - win), WINDOW)
    cs = cos_ref[pl.ds(q0, tq), :]
    sn = sin_ref[pl.ds(q0, tq), :]
    rows = GQA_GROUP * tq
    row = lax.broadcasted_iota(jnp.int32, (rows, win), 0)
    col = lax.broadcasted_iota(jnp.int32, (rows, win), 1)
    rel = (start + col) - (q0 + (row & (tq - 1)))
    mask = jnp.abs(rel) <= WINDOW
    grp = lax.broadcasted_iota(jnp.int32, (rows, 1), 0) // tq
    scale = HEAD_DIM ** -0.5

    for kvh in range(KV_HEADS):
        cols = slice(kvh * HEAD_DIM, (kvh + 1) * HEAD_DIM)
        kw = kr_ref[pl.ds(start, win), cols]
        vw = v_ref[0, pl.ds(start, win), cols]
        qs = []
        sink = jnp.zeros((rows, 1), F32)
        for g in range(GQA_GROUP):
            h = kvh * GQA_GROUP + g
            qq = q_ref[0, :, h * HEAD_DIM:(h + 1) * HEAD_DIM].astype(F32)
            qs.append((_norm_rope(qq, qg_ref[...], cs, sn) * scale).astype(BF16))
            sink = jnp.where(grp == g, sink_ref[h], sink)
        qst = jnp.concatenate(qs, axis=0)
        s = lax.dot_general(qst, kw, (((1,), (1,)), ((), ())), preferred_element_type=F32)
        s = jnp.where(mask, s, -jnp.inf)
        m = jnp.maximum(jnp.max(s, axis=-1, keepdims=True), sink)
        p = jnp.exp(s - m)
        denom = jnp.sum(p, axis=-1, keepdims=True) + jnp.exp(sink - m)
        o = jnp.dot(p.astype(BF16), vw, preferred_element_type=F32) / denom
        for g in range(GQA_GROUP):
            h = kvh * GQA_GROUP + g
            o_ref[0, :, h * HEAD_DIM:(h + 1) * HEAD_DIM] = o[g * tq:(g + 1) * tq].astype(o_ref.dtype)


def _attention(proj3, cos, sin, qg, kg, sink):
    b, seq, _ = proj3.shape
    tq = ATTN_TQ
    kvw = KV_HEADS * HEAD_DIM
    return pl.pallas_call(
        functools.partial(_attn_kernel, tq=tq, seq=seq),
        grid=(b, seq // tq),
        in_specs=[
            pl.BlockSpec(memory_space=pltpu.SMEM),
            pl.BlockSpec((1, tq, ATTN_WIDTH), lambda bi, i: (bi, i, OFF_QA // ATTN_WIDTH)),
            pl.BlockSpec((1, seq, kvw), lambda bi, i: (bi, 0, OFF_KA // kvw)),
            pl.BlockSpec((1, seq, kvw), lambda bi, i: (bi, 0, OFF_VA // kvw)),
            pl.BlockSpec((seq, HEAD_DIM), lambda bi, i: (0, 0)),
            pl.BlockSpec((seq, HEAD_DIM), lambda bi, i: (0, 0)),
            pl.BlockSpec((1, HEAD_DIM), lambda bi, i: (0, 0)),
            pl.BlockSpec((1, HEAD_DIM), lambda bi, i: (0, 0)),
        ],
        out_specs=pl.BlockSpec((1, tq, ATTN_WIDTH), lambda bi, i: (bi, i, 0)),
        out_shape=jax.ShapeDtypeStruct((b, seq, ATTN_WIDTH), BF16),
        scratch_shapes=[pltpu.VMEM((seq, kvw), BF16)],
        compiler_params=_cparams("parallel", "arbitrary"),
        name="swa",
    )(sink, proj3, proj3, proj3, cos, sin, qg, kg)


def _log_sigmoid(z):
    return -(jnp.maximum(-z, 0.0) + jnp.log1p(jnp.exp(-jnp.abs(z))))


def _split3(g):
    hi = g.astype(BF16)
    r1 = g - hi.astype(F32)
    mid = r1.astype(BF16)
    lo = (r1 - mid.astype(F32)).astype(BF16)
    return hi, mid, lo


def _gla_kernel(q_ref, k_ref, v_ref, gate_ref, lr_ref, waf_ref, wab_ref, baf_ref, bab_ref, gn_ref,
                o_ref, gf_ref, gb_ref, acc_ref, st_ref, *, seq):
    c = GLA_CHUNK
    n = seq // c

    def decay_body(t, carry):
        r0 = pl.multiple_of(t * ROPE_ROWS, ROPE_ROWS)
        lr = lr_ref[0, pl.ds(r0, ROPE_ROWS), :].astype(BF16)
        zf = jnp.dot(lr, waf_ref[...], preferred_element_type=F32) + baf_ref[...]
        zb = jnp.dot(lr, wab_ref[...], preferred_element_type=F32) + bab_ref[...]
        gf_ref[pl.ds(r0, ROPE_ROWS), :] = _log_sigmoid(zf) / GLA_GATE_NORMALIZER
        gb_ref[pl.ds(r0, ROPE_ROWS), :] = _log_sigmoid(zb) / GLA_GATE_NORMALIZER
        return carry

    lax.fori_loop(0, seq // ROPE_ROWS, decay_body, 0)

    ri = lax.broadcasted_iota(jnp.int32, (c, c), 0)
    ci = lax.broadcasted_iota(jnp.int32, (c, c), 1)
    lower_incl = ci <= ri
    upper_incl = ci >= ri
    upper_strict = ci > ri
    tri_f = jnp.where(lower_incl, 1.0, 0.0).astype(BF16)
    tri_b = jnp.where(upper_incl, 1.0, 0.0).astype(BF16)
    qscale = GLA_DK ** -0.5

    def direction(idx, g_ref, tri, keep, ref_row, last_row, slot):
        r0 = pl.multiple_of(idx * c, c)
        g = g_ref[pl.ds(r0, c), :]
        hi, mid, lo = _split3(g)
        bcum = (jnp.dot(tri, hi, preferred_element_type=F32)
                + jnp.dot(tri, mid, preferred_element_type=F32)
                + jnp.dot(tri, lo, preferred_element_type=F32))
        b_mid = bcum[ref_row:ref_row + 1, :]
        b_end = bcum[last_row:last_row + 1, :]
        e_q = jnp.exp(bcum - b_mid)
        e_k = jnp.exp(b_mid - bcum)
        q = q_ref[0, pl.ds(r0, c), :].astype(F32) * qscale
        k = k_ref[0, pl.ds(r0, c), :].astype(F32)
        v = v_ref[0, pl.ds(r0, c), :]
        qe = q * e_q
        ke = k * e_k
        a = lax.dot_general(qe.astype(BF16), ke.astype(BF16), (((1,), (1,)), ((), ())),
                            preferred_element_type=F32)
        a = jnp.where(keep, a, 0.0).astype(BF16)
        o = jnp.dot(a, v, preferred_element_type=F32)
        st = st_ref[slot]
        q_in = (qe * jnp.exp(b_mid)).astype(BF16)
        o = o + lax.dot_general(q_in, st.astype(BF16), (((1,), (1,)), ((), ())),
                                preferred_element_type=F32)
        k_out = (ke * jnp.exp(b_end - b_mid)).astype(BF16)
        kv = lax.dot_general(v, k_out, (((0,), (0,)), ((), ())), preferred_element_type=F32)
        st_ref[slot] = st * jnp.exp(b_end) + kv
        return o

    fwd = functools.partial(direction, g_ref=gf_ref, tri=tri_f, keep=lower_incl,
                            ref_row=c // 2, last_row=c - 1, slot=0)
    bwd = functools.partial(direction, g_ref=gb_ref, tri=tri_b, keep=upper_strict,
                            ref_row=c - 1 - c // 2, last_row=0, slot=1)

    def finalize(idx, o):
        r0 = pl.multiple_of(idx * c, c)
        o = o + acc_ref[pl.ds(r0, c), :]
        y = o * lax.rsqrt(jnp.mean(o * o, axis=-1, keepdims=True) + NORM_EPS) * gn_ref[...]
        gate = gate_ref[0, pl.ds(r0, c), :].astype(F32)
        o_ref[0, pl.ds(r0, c), :] = (y * (gate * jax.nn.sigmoid(gate))).astype(o_ref.dtype)

    st_ref[...] = jnp.zeros_like(st_ref)

    def first_half(t, carry):
        tb = n - 1 - t
        acc_ref[pl.ds(pl.multiple_of(t * c, c), c), :] = fwd(t)
        acc_ref[pl.ds(pl.multiple_of(tb * c, c), c), :] = bwd(tb)
        return carry

    def second_half(t, carry):
        tb = n - 1 - t
        finalize(t, fwd(t))
        finalize(tb, bwd(tb))
        return carry

    lax.fori_loop(0, n // 2, first_half, 0)
    lax.fori_loop(n // 2, n, second_half, 0)


def _gla(proj3, lr3, waf, wab, baf, bab, gn):
    b, seq, _ = proj3.shape
    assert (seq // GLA_CHUNK) % 2 == 0 and seq % ROPE_ROWS == 0
    return pl.pallas_call(
        functools.partial(_gla_kernel, seq=seq),
        grid=(b, GLA_HEADS),
        in_specs=[
            pl.BlockSpec((1, seq, GLA_DK), lambda bi, h: (bi, 0, OFF_QG // GLA_DK + h)),
            pl.BlockSpec((1, seq, GLA_DK), lambda bi, h: (bi, 0, OFF_KG // GLA_DK + h)),
            pl.BlockSpec((1, seq, GLA_DV), lambda bi, h: (bi, 0, OFF_VG // GLA_DV + h)),
            pl.BlockSpec((1, seq, GLA_DV), lambda bi, h: (bi, 0, OFF_GG // GLA_DV + h)),
            pl.BlockSpec((1, seq, LR_PAD), lambda bi, h: (bi, 0, 0)),
            pl.BlockSpec((LR_PAD, GLA_DK), lambda bi, h: (0, h)),
            pl.BlockSpec((LR_PAD, GLA_DK), lambda bi, h: (0, h)),
            pl.BlockSpec((1, GLA_DK), lambda bi, h: (0, h)),
            pl.BlockSpec((1, GLA_DK), lambda bi, h: (0, h)),
            pl.BlockSpec((1, GLA_DV), lambda bi, h: (0, 0)),
        ],
        out_specs=pl.BlockSpec((1, seq, GLA_DV), lambda bi, h: (bi, 0, h)),
        out_shape=jax.ShapeDtypeStruct((b, seq, GLA_WIDTH), BF16),
        scratch_shapes=[
            pltpu.VMEM((seq, GLA_DK), F32),
            pltpu.VMEM((seq, GLA_DK), F32),
            pltpu.VMEM((seq, GLA_DV), F32),
            pltpu.VMEM((2, GLA_DV, GLA_DK), F32),
        ],
        compiler_params=_cparams("parallel", "arbitrary"),
        name="gla",
    )(proj3, proj3, proj3, proj3, lr3, waf, wab, baf, bab, gn)


def _outproj_kernel(oa_ref, og_ref, x_ref, w_ref, g_ref, x1_ref, h2_ref):
    wa = oa_ref.shape[1]
    acc = jnp.dot(oa_ref[...], w_ref[:wa, :], preferred_element_type=F32)
    acc = acc + jnp.dot(og_ref[...], w_ref[wa:, :], preferred_element_type=F32)
    x1 = x_ref[...] + acc
    x1_ref[...] = x1
    ms = jnp.mean(x1 * x1, axis=-1, keepdims=True)
    h2_ref[...] = (x1 * lax.rsqrt(ms + NORM_EPS) * g_ref[...]).astype(h2_ref.dtype)


def _outproj(oa, og, x2, w_out, g2):
    m, d = x2.shape
    tm = OUTPROJ_TM
    wa, wg = oa.shape[1], og.shape[1]
    return pl.pallas_call(
        _outproj_kernel,
        grid=(m // tm,),
        in_specs=[
            pl.BlockSpec((tm, wa), lambda i: (i, 0)),
            pl.BlockSpec((tm, wg), lambda i: (i, 0)),
            pl.BlockSpec((tm, d), lambda i: (i, 0)),
            pl.BlockSpec((wa + wg, d), lambda i: (0, 0)),
            pl.BlockSpec((1, d), lambda i: (0, 0)),
        ],
        out_specs=[
            pl.BlockSpec((tm, d), lambda i: (i, 0)),
            pl.BlockSpec((tm, d), lambda i: (i, 0)),
        ],
        out_shape=[
            jax.ShapeDtypeStruct((m, d), F32),
            jax.ShapeDtypeStruct((m, d), BF16),
        ],
        compiler_params=_cparams("parallel"),
        name="outproj",
    )(oa, og, x2, w_out, g2)


def _ffn_kernel(h_ref, hp_ref, hn_ref, x1_ref, wg_ref, wv_ref, cwg_ref, cwv_ref, cbg_ref, cbv_ref,
                wd_ref, o_ref, hext_ref, acc_ref, *, tm, tiles_per_seq):
    i = pl.program_id(0)
    f = pl.program_id(1)

    @pl.when(f == 0)
    def _():
        hext_ref[:tm, :] = h_ref[...]
        pos = i % tiles_per_seq
        r = lax.broadcasted_iota(jnp.int32, (HALO, 1), 0)
        take_next = (r == 0) & (pos != tiles_per_seq - 1)
        take_prev = (r == HALO - 1) & (pos != 0)
        zero = jnp.zeros_like(hn_ref[...])
        hext_ref[tm:, :] = jnp.where(take_next, hn_ref[...], jnp.where(take_prev, hp_ref[...], zero))
        acc_ref[...] = jnp.zeros_like(acc_ref)

    hext = hext_ref[...]
    rows = tm + HALO

    def conv(u, cw_ref, cb_ref):
        prev = pltpu.roll(u, 1, 0)
        nxt = pltpu.roll(u, rows - 1, 0)
        y = prev * cw_ref[0:1, :] + u * cw_ref[1:2, :] + nxt * cw_ref[2:3, :] + cb_ref[...]
        return y[:tm]

    ug = conv(jnp.dot(hext, wg_ref[...], preferred_element_type=F32), cwg_ref, cbg_ref)
    uv = conv(jnp.dot(hext, wv_ref[...], preferred_element_type=F32), cwv_ref, cbv_ref)
    act = (ug * jax.nn.sigmoid(ug) * uv).astype(BF16)
    acc_ref[...] += jnp.dot(act, wd_ref[...], preferred_element_type=F32)

    @pl.when(f == pl.num_programs(1) - 1)
    def _():
        o_ref[...] = x1_ref[...] + acc_ref[...]


def _ffn(h2, x1, w_up, conv_w, conv_b, w_down, seq):
    m, d = h2.shape
    dff = w_down.shape[0]
    tm, tf = FFN_TM, FFN_TF
    nf = dff // tf
    hb = tm // HALO
    last_halo = m // HALO - 1
    return pl.pallas_call(
        functools.partial(_ffn_kernel, tm=tm, tiles_per_seq=seq // tm),
        grid=(m // tm, nf),
        in_specs=[
            pl.BlockSpec((tm, d), lambda i, f: (i, 0)),
            pl.BlockSpec((HALO, d), lambda i, f: (jnp.maximum(i * hb - 1, 0), 0)),
            pl.BlockSpec((HALO, d), lambda i, f: (jnp.minimum((i + 1) * hb, last_halo), 0)),
            pl.BlockSpec((tm, d), lambda i, f: (i, 0)),
            pl.BlockSpec((d, tf), lambda i, f: (0, f)),
            pl.BlockSpec((d, tf), lambda i, f: (0, nf + f)),
            pl.BlockSpec((3, tf), lambda i, f: (0, f)),
            pl.BlockSpec((3, tf), lambda i, f: (0, nf + f)),
            pl.BlockSpec((1, tf), lambda i, f: (0, f)),
            pl.BlockSpec((1, tf), lambda i, f: (0, nf + f)),
            pl.BlockSpec((tf, d), lambda i, f: (f, 0)),
        ],
        out_specs=pl.BlockSpec((tm, d), lambda i, f: (i, 0)),
        out_shape=jax.ShapeDtypeStruct((m, d), F32),
        scratch_shapes=[
            pltpu.VMEM((tm + HALO, d), BF16),
            pltpu.VMEM((tm, d), F32),
        ],
        compiler_params=_cparams("parallel", "arbitrary"),
        name="convffn",
    )(h2, h2, h2, x1, w_up, w_up, conv_w, conv_w, conv_b, conv_b, w_down)


def _rope_tables(seq):
    half = HEAD_DIM // 2
    inv = 1.0 / (ROPE_THETA ** (jnp.arange(half, dtype=F32) / half))
    ang = jnp.arange(seq, dtype=F32)[:, None] * inv[None, :]
    cos = jnp.cos(ang)
    sin = jnp.sin(ang)
    return jnp.concatenate([cos, cos], axis=-1), jnp.concatenate([-sin, sin], axis=-1)


def _pad_decay_weight(wa2, row0):
    out = jnp.zeros((LR_PAD, wa2.shape[1]), F32)
    return lax.dynamic_update_slice(out, wa2, (row0, 0)).astype(BF16)


def _layer(x, norm1_g, w_in, q_norm_g, k_norm_g, sink, wa2_f, ba_f, wa2_b, ba_b, gla_norm_g,
           w_out, norm2_g, w_up, conv_w, conv_b, w_down):
    b, seq, d = x.shape
    m = b * seq
    x2 = x.reshape(m, d)
    w_main = w_in[:, :MAIN_WIDTH].astype(BF16)
    w_lr = jnp.pad(w_in[:, MAIN_WIDTH:], ((0, 0), (0, LR_PAD - 2 * GLA_DECAY_RANK))).astype(BF16)
    proj, lr = _inproj(x2, norm1_g.reshape(1, d), w_main, w_lr)
    proj3 = proj.reshape(b, seq, MAIN_WIDTH)
    cos, sin = _rope_tables(seq)
    o_attn = _attention(proj3, cos, sin, q_norm_g.reshape(1, HEAD_DIM), k_norm_g.reshape(1, HEAD_DIM), sink)
    o_gla = _gla(proj3, lr.reshape(b, seq, LR_PAD),
                 _pad_decay_weight(wa2_f, 0), _pad_decay_weight(wa2_b, GLA_DECAY_RANK),
                 ba_f.reshape(1, -1), ba_b.reshape(1, -1), gla_norm_g.reshape(1, GLA_DV))
    x1, h2 = _outproj(o_attn.reshape(m, ATTN_WIDTH), o_gla.reshape(m, GLA_WIDTH), x2,
                      w_out.astype(BF16), norm2_g.reshape(1, d))
    out = _ffn(h2, x1, w_up.astype(BF16), conv_w, conv_b.reshape(1, -1), w_down.astype(BF16), seq)
    return out.reshape(b, seq, d)


def kernel(x, norm1_g, w_in, attn_q_norm_g, attn_k_norm_g, attn_sink, gla_wa2_fwd, gla_ba_fwd, gla_wa2_bwd, gla_ba_bwd, gla_out_norm_g, w_out, norm2_g, w_up, conv_w, conv_b, w_down):
    for l in range(w_in.shape[0]):
        x = _layer(x, norm1_g[l], w_in[l], attn_q_norm_g[l], attn_k_norm_g[l], attn_sink[l],
                   gla_wa2_fwd[l], gla_ba_fwd[l], gla_wa2_bwd[l], gla_ba_bwd[l], gla_out_norm_g[l],
                   w_out[l], norm2_g[l], w_up[l], conv_w[l], conv_b[l], w_down[l])
    return x
```

```python
import functools

import jax
import jax.numpy as jnp
from jax import lax
from jax.experimental import pallas as pl
from jax.experimental.pallas import tpu as pltpu

F32 = jnp.float32
BF16 = jnp.bfloat16

HEAD_DIM = 128
KV_HEADS = 2
GQA_GROUP = 4
ATTN_HEADS = KV_HEADS * GQA_GROUP
ATTN_WIDTH = ATTN_HEADS * HEAD_DIM
WINDOW = 128
ROPE_THETA = 10000.0
GLA_HEADS = 4
GLA_DK = 128
GLA_DV = 256
GLA_KEY_WIDTH = GLA_HEADS * GLA_DK
GLA_WIDTH = GLA_HEADS * GLA_DV
GLA_DECAY_RANK = 16
GLA_GATE_NORMALIZER = 16.0
GLA_CHUNK = 64
GLA_BLOCK = 256
NORM_EPS = 1e-6

OFF_QA = 0
OFF_KA = OFF_QA + ATTN_WIDTH
OFF_VA = OFF_KA + KV_HEADS * HEAD_DIM
OFF_QG = OFF_VA + KV_HEADS * HEAD_DIM
OFF_KG = OFF_QG + GLA_KEY_WIDTH
OFF_VG = OFF_KG + GLA_KEY_WIDTH
OFF_GG = OFF_VG + GLA_WIDTH
OFF_LR = OFF_GG + GLA_WIDTH
MAIN_WIDTH = OFF_LR
LR_PAD = 128

VMEM_LIMIT_BYTES = 56 * 1024 * 1024
INPROJ_TM = 512
INPROJ_TN = 1536
ATTN_TQ = 256
ROPE_ROWS = 512
OUTPROJ_TM = 512
FFN_TM = 512
FFN_TF = 512
HALO = 16


def _cparams(*sem):
    return pltpu.CompilerParams(dimension_semantics=sem, vmem_limit_bytes=VMEM_LIMIT_BYTES)


def _inproj_kernel(x_ref, g_ref, w_ref, wlr_ref, o_ref, lr_ref, h_ref):
    @pl.when(pl.program_id(1) == 0)
    def _():
        x = x_ref[...]
        ms = jnp.mean(x * x, axis=-1, keepdims=True)
        h = (x * lax.rsqrt(ms + NORM_EPS) * g_ref[...]).astype(BF16)
        h_ref[...] = h
        lr_ref[...] = jnp.dot(h, wlr_ref[...], preferred_element_type=F32)

    o_ref[...] = jnp.dot(h_ref[...], w_ref[...], preferred_element_type=F32).astype(o_ref.dtype)


def _inproj(x2, g1, w_main, w_lr):
    m, d = x2.shape
    tm, tn = INPROJ_TM, INPROJ_TN
    return pl.pallas_call(
        _inproj_kernel,
        grid=(m // tm, MAIN_WIDTH // tn),
        in_specs=[
            pl.BlockSpec((tm, d), lambda i, j: (i, 0)),
            pl.BlockSpec((1, d), lambda i, j: (0, 0)),
            pl.BlockSpec((d, tn), lambda i, j: (0, j)),
            pl.BlockSpec((d, LR_PAD), lambda i, j: (0, 0)),
        ],
        out_specs=[
            pl.BlockSpec((tm, tn), lambda i, j: (i, j)),
            pl.BlockSpec((tm, LR_PAD), lambda i, j: (i, 0)),
        ],
        out_shape=[
            jax.ShapeDtypeStruct((m, MAIN_WIDTH), BF16),
            jax.ShapeDtypeStruct((m, LR_PAD), F32),
        ],
        scratch_shapes=[pltpu.VMEM((tm, d), BF16)],
        compiler_params=_cparams("parallel", "arbitrary"),
        name="inproj",
    )(x2, g1, w_main, w_lr)


def _norm_rope(t, gain, cos, sin):
    t = t * lax.rsqrt(jnp.mean(t * t, axis=-1, keepdims=True) + NORM_EPS) * gain
    return t * cos + pltpu.roll(t, HEAD_DIM // 2, 1) * sin


def _attn_kernel(sink_ref, q_ref, k_ref, v_ref, cos_ref, sin_ref, qg_ref, kg_ref, o_ref, kr_ref,
                 *, tq, seq):
    i = pl.program_id(1)

    @pl.when(i == 0)
    def _():
        def body(c, carry):
            r0 = pl.multiple_of(c * ROPE_ROWS, ROPE_ROWS)
            cs = cos_ref[pl.ds(r0, ROPE_ROWS), :]
            sn = sin_ref[pl.ds(r0, ROPE_ROWS), :]
            for h in range(KV_HEADS):
                cols = slice(h * HEAD_DIM, (h + 1) * HEAD_DIM)
                kk = k_ref[0, pl.ds(r0, ROPE_ROWS), cols].astype(F32)
                kr_ref[pl.ds(r0, ROPE_ROWS), cols] = _norm_rope(kk, kg_ref[...], cs, sn).astype(BF16)
            return carry

        lax.fori_loop(0, seq // ROPE_ROWS, body, 0)

    win = tq + 2 * WINDOW
    q0 = pl.multiple_of(i * tq, tq)
    start = pl.multiple_of(jnp.clip(q0 - WINDOW, 0, seq - win), WINDOW)
    cs = cos_ref[pl.ds(q0, tq), :]
    sn = sin_ref[pl.ds(q0, tq), :]
    rows = GQA_GROUP * tq
    row = lax.broadcasted_iota(jnp.int32, (rows, win), 0)
    col = lax.broadcasted_iota(jnp.int32, (rows, win), 1)
    rel = (start + col) - (q0 + (row & (tq - 1)))
    mask = jnp.abs(rel) <= WINDOW
    grp = lax.broadcasted_iota(jnp.int32, (rows, 1), 0) // tq
    scale = HEAD_DIM ** -0.5

    for kvh in range(KV_HEADS):
        cols = slice(kvh * HEAD_DIM, (kvh + 1) * HEAD_DIM)
        kw = kr_ref[pl.ds(start, win), cols]
        vw = v_ref[0, pl.ds(start, win), cols]
        qs = []
        sink = jnp.zeros((rows, 1), F32)
        for g in range(GQA_GROUP):
            h = kvh * GQA_GROUP + g
            qq = q_ref[0, :, h * HEAD_DIM:(h + 1) * HEAD_DIM].astype(F32)
            qs.append((_norm_rope(qq, qg_ref[...], cs, sn) * scale).astype(BF16))
            sink = jnp.where(grp == g, sink_ref[h], sink)
        qst = jnp.concatenate(qs, axis=0)
        s = lax.dot_general(qst, kw, (((1,), (1,)), ((), ())), preferred_element_type=F32)
        s = jnp.where(mask, s, -jnp.inf)
        m = jnp.maximum(jnp.max(s, axis=-1, keepdims=True), sink)
        p = jnp.exp(s - m)
        denom = jnp.sum(p, axis=-1, keepdims=True) + jnp.exp(sink - m)
        o = jnp.dot(p.astype(BF16), vw, preferred_element_type=F32) / denom
        for g in range(GQA_GROUP):
            h = kvh * GQA_GROUP + g
            o_ref[0, :, h * HEAD_DIM:(h + 1) * HEAD_DIM] = o[g * tq:(g + 1) * tq].astype(o_ref.dtype)


def _attention(proj3, cos, sin, qg, kg, sink):
    b, seq, _ = proj3.shape
    tq = ATTN_TQ
    kvw = KV_HEADS * HEAD_DIM
    return pl.pallas_call(
        functools.partial(_attn_kernel, tq=tq, seq=seq),
        grid=(b, seq // tq),
        in_specs=[
            pl.BlockSpec(memory_space=pltpu.SMEM),
            pl.BlockSpec((1, tq, ATTN_WIDTH), lambda bi, i: (bi, i, OFF_QA // ATTN_WIDTH)),
            pl.BlockSpec((1, seq, kvw), lambda bi, i: (bi, 0, OFF_KA // kvw)),
            pl.BlockSpec((1, seq, kvw), lambda bi, i: (bi, 0, OFF_VA // kvw)),
            pl.BlockSpec((seq, HEAD_DIM), lambda bi, i: (0, 0)),
            pl.BlockSpec((seq, HEAD_DIM), lambda bi, i: (0, 0)),
            pl.BlockSpec((1, HEAD_DIM), lambda bi, i: (0, 0)),
            pl.BlockSpec((1, HEAD_DIM), lambda bi, i: (0, 0)),
        ],
        out_specs=pl.BlockSpec((1, tq, ATTN_WIDTH), lambda bi, i: (bi, i, 0)),
        out_shape=jax.ShapeDtypeStruct((b, seq, ATTN_WIDTH), BF16),
        scratch_shapes=[pltpu.VMEM((seq, kvw), BF16)],
        compiler_params=_cparams("parallel", "arbitrary"),
        name="swa",
    )(sink, proj3, proj3, proj3, cos, sin, qg, kg)


def _log_sigmoid(z):
    return -(jnp.maximum(-z, 0.0) + jnp.log1p(jnp.exp(-jnp.abs(z))))


def _split2(g):
    hi = g.astype(BF16)
    lo = (g - hi.astype(F32)).astype(BF16)
    return hi, lo


def _gla_kernel(q_ref, k_ref, v_ref, gate_ref, lr_ref, wa_ref, ba_ref, gn_ref,
                o_ref, qin_ref, kv_ref, dec_ref, acc_ref, st_ref, *, seq):
    c = GLA_CHUNK
    n = seq // c
    blk = GLA_BLOCK
    cpb = blk // c

    ri = lax.broadcasted_iota(jnp.int32, (blk, blk), 0)
    ci = lax.broadcasted_iota(jnp.int32, (blk, blk), 1)
    same = (ri // c) == (ci // c)
    lower_incl = same & (ci <= ri)
    upper_incl = same & (ci >= ri)
    upper_strict = same & (ci > ri)
    tri_f = jnp.where(lower_incl, 1.0, 0.0).astype(BF16)
    tri_b = jnp.where(upper_incl, 1.0, 0.0).astype(BF16)
    qscale = GLA_DK ** -0.5

    def intra(t, carry):
        r0 = pl.multiple_of(t * blk, blk)
        lr = lr_ref[0, pl.ds(r0, blk), :].astype(BF16)
        q3 = (q_ref[0, pl.ds(r0, blk), :].astype(F32) * qscale).reshape(cpb, c, GLA_DK)
        k3 = k_ref[0, pl.ds(r0, blk), :].astype(F32).reshape(cpb, c, GLA_DK)
        v = v_ref[0, pl.ds(r0, blk), :]

        z = jnp.dot(lr, wa_ref[...], preferred_element_type=F32) + ba_ref[...]
        g_hi, g_lo = _split2(_log_sigmoid(z) / GLA_GATE_NORMALIZER)

        def one_direction(d, tri, keep, ref_row, last_row, slot):
            cols = slice(d * GLA_DK, (d + 1) * GLA_DK)
            parts = jnp.concatenate([g_hi[:, cols], g_lo[:, cols]], axis=1)
            bsum = jnp.dot(tri, parts, preferred_element_type=F32)
            bcum = (bsum[:, :GLA_DK] + bsum[:, GLA_DK:]).reshape(cpb, c, GLA_DK)
            b_mid = bcum[:, ref_row:ref_row + 1, :]
            b_end = bcum[:, last_row:last_row + 1, :]
            qe = q3 * jnp.exp(bcum - b_mid)
            ke = k3 * jnp.exp(b_mid - bcum)
            qin_ref[slot, pl.ds(r0, blk), :] = (qe * jnp.exp(b_mid)).astype(BF16).reshape(blk, GLA_DK)
            kout = (ke * jnp.exp(b_end - b_mid)).astype(BF16)
            for j in range(cpb):
                kv_ref[slot, t * cpb + j] = lax.dot_general(
                    v[j * c:(j + 1) * c], kout[j], (((0,), (0,)), ((), ())), preferred_element_type=F32)
            dec = jnp.broadcast_to(jnp.exp(b_end), (cpb, 8, GLA_DK)).reshape(cpb * 8, GLA_DK)
            dec_ref[slot, pl.ds(pl.multiple_of(t * (cpb * 8), cpb * 8), cpb * 8), :] = dec
            a = lax.dot_general(qe.astype(BF16).reshape(blk, GLA_DK), ke.astype(BF16).reshape(blk, GLA_DK),
                                (((1,), (1,)), ((), ())), preferred_element_type=F32)
            a = jnp.where(keep, a, 0.0).astype(BF16)
            return jnp.dot(a, v, preferred_element_type=F32)

        o = one_direction(0, tri_f, lower_incl, c // 2, c - 1, 0)
        o = o + one_direction(1, tri_b, upper_strict, c - 1 - c // 2, 0, 1)
        acc_ref[pl.ds(r0, blk), :] = o
        return carry

    lax.fori_loop(0, seq // blk, intra, 0, unroll=2)

    def step(idx, slot):
        r0 = pl.multiple_of(idx * c, c)
        st = st_ref[slot]
        o = lax.dot_general(qin_ref[slot, pl.ds(r0, c), :], st.astype(BF16), (((1,), (1,)), ((), ())),
                            preferred_element_type=F32)
        dec = dec_ref[slot, pl.ds(pl.multiple_of(idx * 8, 8), 8), :]
        st_ref[slot] = ((st.reshape(GLA_DV // 8, 8, GLA_DK) * dec[None]).reshape(GLA_DV, GLA_DK)
                        + kv_ref[slot, idx])
        return r0, o

    def accumulate(r0, o):
        acc_ref[pl.ds(r0, c), :] += o

    def finalize(r0, o):
        o = o + acc_ref[pl.ds(r0, c), :]
        y = o * lax.rsqrt(jnp.mean(o * o, axis=-1, keepdims=True) + NORM_EPS) * gn_ref[...]
        gate = gate_ref[0, pl.ds(r0, c), :].astype(F32)
        o_ref[0, pl.ds(r0, c), :] = (y * (gate * jax.nn.sigmoid(gate))).astype(o_ref.dtype)

    st_ref[...] = jnp.zeros_like(st_ref)

    def first_half(t, carry):
        accumulate(*step(t, 0))
        accumulate(*step(n - 1 - t, 1))
        return carry

    def second_half(t, carry):
        finalize(*step(t, 0))
        finalize(*step(n - 1 - t, 1))
        return carry

    lax.fori_loop(0, n // 2, first_half, 0, unroll=4)
    lax.fori_loop(n // 2, n, second_half, 0, unroll=4)


def _gla(proj3, lr3, wa, ba, gn):
    b, seq, _ = proj3.shape
    assert (seq // GLA_CHUNK) % 8 == 0 and seq % GLA_BLOCK == 0
    return pl.pallas_call(
        functools.partial(_gla_kernel, seq=seq),
        grid=(b, GLA_HEADS),
        in_specs=[
            pl.BlockSpec((1, seq, GLA_DK), lambda bi, h: (bi, 0, OFF_QG // GLA_DK + h)),
            pl.BlockSpec((1, seq, GLA_DK), lambda bi, h: (bi, 0, OFF_KG // GLA_DK + h)),
            pl.BlockSpec((1, seq, GLA_DV), lambda bi, h: (bi, 0, OFF_VG // GLA_DV + h)),
            pl.BlockSpec((1, seq, GLA_DV), lambda bi, h: (bi, 0, OFF_GG // GLA_DV + h)),
            pl.BlockSpec((1, seq, LR_PAD), lambda bi, h: (bi, 0, 0)),
            pl.BlockSpec((LR_PAD, 2 * GLA_DK), lambda bi, h: (0, h)),
            pl.BlockSpec((1, 2 * GLA_DK), lambda bi, h: (0, h)),
            pl.BlockSpec((1, GLA_DV), lambda bi, h: (0, 0)),
        ],
        out_specs=pl.BlockSpec((1, seq, GLA_DV), lambda bi, h: (bi, 0, h)),
        out_shape=jax.ShapeDtypeStruct((b, seq, GLA_WIDTH), BF16),
        scratch_shapes=[
            pltpu.VMEM((2, seq, GLA_DK), BF16),
            pltpu.VMEM((2, seq // GLA_CHUNK, GLA_DV, GLA_DK), F32),
            pltpu.VMEM((2, seq // GLA_CHUNK * 8, GLA_DK), F32),
            pltpu.VMEM((seq, GLA_DV), F32),
            pltpu.VMEM((2, GLA_DV, GLA_DK), F32),
        ],
        compiler_params=_cparams("parallel", "arbitrary"),
        name="gla",
    )(proj3, proj3, proj3, proj3, lr3, wa, ba, gn)


def _outproj_kernel(oa_ref, og_ref, x_ref, w_ref, g_ref, x1_ref, h2_ref):
    wa = oa_ref.shape[1]
    acc = jnp.dot(oa_ref[...], w_ref[:wa, :], preferred_element_type=F32)
    acc = acc + jnp.dot(og_ref[...], w_ref[wa:, :], preferred_element_type=F32)
    x1 = x_ref[...] + acc
    x1_ref[...] = x1
    ms = jnp.mean(x1 * x1, axis=-1, keepdims=True)
    h2_ref[...] = (x1 * lax.rsqrt(ms + NORM_EPS) * g_ref[...]).astype(h2_ref.dtype)


def _outproj(oa, og, x2, w_out, g2):
    m, d = x2.shape
    tm = OUTPROJ_TM
    wa, wg = oa.shape[1], og.shape[1]
    return pl.pallas_call(
        _outproj_kernel,
        grid=(m // tm,),
        in_specs=[
            pl.BlockSpec((tm, wa), lambda i: (i, 0)),
            pl.BlockSpec((tm, wg), lambda i: (i, 0)),
            pl.BlockSpec((tm, d), lambda i: (i, 0)),
            pl.BlockSpec((wa + wg, d), lambda i: (0, 0)),
            pl.BlockSpec((1, d), lambda i: (0, 0)),
        ],
        out_specs=[
            pl.BlockSpec((tm, d), lambda i: (i, 0)),
            pl.BlockSpec((tm, d), lambda i: (i, 0)),
        ],
        out_shape=[
            jax.ShapeDtypeStruct((m, d), F32),
            jax.ShapeDtypeStruct((m, d), BF16),
        ],
        compiler_params=_cparams("parallel"),
        name="outproj",
    )(oa, og, x2, w_out, g2)


def _ffn_kernel(h_ref, hp_ref, hn_ref, x1_ref, wg_ref, wv_ref, cwg_ref, cwv_ref, cbg_ref, cbv_ref,
                wd_ref, o_ref, hext_ref, acc_ref, *, tm, tiles_per_seq):
    i = pl.program_id(0)
    f = pl.program_id(1)

    @pl.when(f == 0)
    def _():
        hext_ref[:tm, :] = h_ref[...]
        pos = i % tiles_per_seq
        r = lax.broadcasted_iota(jnp.int32, (HALO, 1), 0)
        take_next = (r == 0) & (pos != tiles_per_seq - 1)
        take_prev = (r == HALO - 1) & (pos != 0)
        zero = jnp.zeros_like(hn_ref[...])
        hext_ref[tm:, :] = jnp.where(take_next, hn_ref[...], jnp.where(take_prev, hp_ref[...], zero))
        acc_ref[...] = jnp.zeros_like(acc_ref)

    hext = hext_ref[...]
    rows = tm + HALO

    def conv(u, cw_ref, cb_ref):
        prev = pltpu.roll(u, 1, 0)
        nxt = pltpu.roll(u, rows - 1, 0)
        y = prev * cw_ref[0:1, :] + u * cw_ref[1:2, :] + nxt * cw_ref[2:3, :] + cb_ref[...]
        return y[:tm]

    ug = conv(jnp.dot(hext, wg_ref[...], preferred_element_type=F32), cwg_ref, cbg_ref)
    uv = conv(jnp.dot(hext, wv_ref[...], preferred_element_type=F32), cwv_ref, cbv_ref)
    act = (ug * jax.nn.sigmoid(ug) * uv).astype(BF16)
    acc_ref[...] += jnp.dot(act, wd_ref[...], preferred_element_type=F32)

    @pl.when(f == pl.num_programs(1) - 1)
    def _():
        o_ref[...] = x1_ref[...] + acc_ref[...]


def _ffn(h2, x1, w_up, conv_w, conv_b, w_down, seq):
    m, d = h2.shape
    dff = w_down.shape[0]
    tm, tf = FFN_TM, FFN_TF
    nf = dff // tf
    hb = tm // HALO
    last_halo = m // HALO - 1
    return pl.pallas_call(
        functools.partial(_ffn_kernel, tm=tm, tiles_per_seq=seq // tm),
        grid=(m // tm, nf),
        in_specs=[
            pl.BlockSpec((tm, d), lambda i, f: (i, 0)),
            pl.BlockSpec((HALO, d), lambda i, f: (jnp.maximum(i * hb - 1, 0), 0)),
            pl.BlockSpec((HALO, d), lambda i, f: (jnp.minimum((i + 1) * hb, last_halo), 0)),
            pl.BlockSpec((tm, d), lambda i, f: (i, 0)),
            pl.BlockSpec((d, tf), lambda i, f: (0, f)),
            pl.BlockSpec((d, tf), lambda i, f: (0, nf + f)),
            pl.BlockSpec((3, tf), lambda i, f: (0, f)),
            pl.BlockSpec((3, tf), lambda i, f: (0, nf + f)),
            pl.BlockSpec((1, tf), lambda i, f: (0, f)),
            pl.BlockSpec((1, tf), lambda i, f: (0, nf + f)),
            pl.BlockSpec((tf, d), lambda i, f: (f, 0)),
        ],
        out_specs=pl.BlockSpec((tm, d), lambda i, f: (i, 0)),
        out_shape=jax.ShapeDtypeStruct((m, d), F32),
        scratch_shapes=[
            pltpu.VMEM((tm + HALO, d), BF16),
            pltpu.VMEM((tm, d), F32),
        ],
        compiler_params=_cparams("parallel", "arbitrary"),
        name="convffn",
    )(h2, h2, h2, x1, w_up, w_up, conv_w, conv_w, conv_b, conv_b, w_down)


def _rope_tables(seq):
    half = HEAD_DIM // 2
    inv = 1.0 / (ROPE_THETA ** (jnp.arange(half, dtype=F32) / half))
    ang = jnp.arange(seq, dtype=F32)[:, None] * inv[None, :]
    cos = jnp.cos(ang)
    sin = jnp.sin(ang)
    return jnp.concatenate([cos, cos], axis=-1), jnp.concatenate([-sin, sin], axis=-1)


def _decay_params(wa2_f, ba_f, wa2_b, ba_b):
    r = GLA_DECAY_RANK
    wf = jnp.pad(wa2_f, ((0, LR_PAD - r), (0, 0))).reshape(LR_PAD, GLA_HEADS, 1, GLA_DK)
    wb = jnp.pad(wa2_b, ((r, LR_PAD - 2 * r), (0, 0))).reshape(LR_PAD, GLA_HEADS, 1, GLA_DK)
    wa = jnp.concatenate([wf, wb], axis=2).reshape(LR_PAD, 2 * GLA_KEY_WIDTH).astype(BF16)
    ba = jnp.concatenate([ba_f.reshape(GLA_HEADS, 1, GLA_DK), ba_b.reshape(GLA_HEADS, 1, GLA_DK)],
                         axis=1).reshape(1, 2 * GLA_KEY_WIDTH)
    return wa, ba


def _layer(x, norm1_g, w_in, q_norm_g, k_norm_g, sink, wa2_f, ba_f, wa2_b, ba_b, gla_norm_g,
           w_out, norm2_g, w_up, conv_w, conv_b, w_down):
    b, seq, d = x.shape
    m = b * seq
    x2 = x.reshape(m, d)
    w_main = w_in[:, :MAIN_WIDTH].astype(BF16)
    w_lr = jnp.pad(w_in[:, MAIN_WIDTH:], ((0, 0), (0, LR_PAD - 2 * GLA_DECAY_RANK))).astype(BF16)
    proj, lr = _inproj(x2, norm1_g.reshape(1, d), w_main, w_lr)
    proj3 = proj.reshape(b, seq, MAIN_WIDTH)
    cos, sin = _rope_tables(seq)
    o_attn = _attention(proj3, cos, sin, q_norm_g.reshape(1, HEAD_DIM), k_norm_g.reshape(1, HEAD_DIM), sink)
    wa, ba = _decay_params(wa2_f, ba_f, wa2_b, ba_b)
    o_gla = _gla(proj3, lr.reshape(b, seq, LR_PAD), wa, ba, gla_norm_g.reshape(1, GLA_DV))
    x1, h2 = _outproj(o_attn.reshape(m, ATTN_WIDTH), o_gla.reshape(m, GLA_WIDTH), x2,
                      w_out.astype(BF16), norm2_g.reshape(1, d))
    out = _ffn(h2, x1, w_up.astype(BF16), conv_w, conv_b.reshape(1, -1), w_down.astype(BF16), seq)
    return out.reshape(b, seq, d)


def kernel(x, norm1_g, w_in, attn_q_norm_g, attn_k_norm_g, attn_sink, gla_wa2_fwd, gla_ba_fwd, gla_wa2_bwd, gla_ba_bwd, gla_out_norm_g, w_out, norm2_g, w_up, conv_w, conv_b, w_down):
    for l in range(w_in.shape[0]):
        x = _layer(x, norm1_g[l], w_in[l], attn_q_norm_g[l], attn_k_norm_g[l], attn_sink[l],
                   gla_wa2_fwd[l], gla_ba_fwd[l], gla_wa2_bwd[l], gla_ba_bwd[l], gla_out_norm_g[l],
                   w_out[l], norm2_g[l], w_up[l], conv_w[l], conv_b[l], w_down[l])
    return x
```

```python
import functools

import jax
import jax.numpy as jnp
from jax import lax
from jax.experimental import pallas as pl
from jax.experimental.pallas import tpu as pltpu

F32 = jnp.float32
BF16 = jnp.bfloat16

HEAD_DIM = 128
KV_HEADS = 2
GQA_GROUP = 4
ATTN_HEADS = KV_HEADS * GQA_GROUP
ATTN_WIDTH = ATTN_HEADS * HEAD_DIM
WINDOW = 128
ROPE_THETA = 10000.0
GLA_HEADS = 4
GLA_DK = 128
GLA_DV = 256
GLA_KEY_WIDTH = GLA_HEADS * GLA_DK
GLA_WIDTH = GLA_HEADS * GLA_DV
GLA_DECAY_RANK = 16
GLA_GATE_NORMALIZER = 16.0
GLA_CHUNK = 64
GLA_BLOCK = 256
NORM_EPS = 1e-6

OFF_QA = 0
OFF_KA = OFF_QA + ATTN_WIDTH
OFF_VA = OFF_KA + KV_HEADS * HEAD_DIM
OFF_QG = OFF_VA + KV_HEADS * HEAD_DIM
OFF_KG = OFF_QG + GLA_KEY_WIDTH
OFF_VG = OFF_KG + GLA_KEY_WIDTH
OFF_GG = OFF_VG + GLA_WIDTH
OFF_LR = OFF_GG + GLA_WIDTH
MAIN_WIDTH = OFF_LR
LR_PAD = 128

VMEM_LIMIT_BYTES = 56 * 1024 * 1024
INPROJ_TM = 1024
INPROJ_TN = 1536
ATTN_TQ = 128
ROPE_ROWS = 512
OUTPROJ_TM = 512
FFN_TM = 512
FFN_TF = 512
HALO = 16


def _cparams(*sem):
    return pltpu.CompilerParams(dimension_semantics=sem, vmem_limit_bytes=VMEM_LIMIT_BYTES)


def _inproj_kernel(x_ref, g_ref, w_ref, wlr_ref, o_ref, lr_ref, h_ref):
    @pl.when(pl.program_id(1) == 0)
    def _():
        x = x_ref[...]
        ms = jnp.mean(x * x, axis=-1, keepdims=True)
        h = (x * lax.rsqrt(ms + NORM_EPS) * g_ref[...]).astype(BF16)
        h_ref[...] = h
        lr_ref[...] = jnp.dot(h, wlr_ref[...], preferred_element_type=F32)

    o_ref[...] = jnp.dot(h_ref[...], w_ref[...], preferred_element_type=F32).astype(o_ref.dtype)


def _inproj(x2, g1, w_main, w_lr):
    m, d = x2.shape
    tm, tn = INPROJ_TM, INPROJ_TN
    return pl.pallas_call(
        _inproj_kernel,
        grid=(m // tm, MAIN_WIDTH // tn),
        in_specs=[
            pl.BlockSpec((tm, d), lambda i, j: (i, 0)),
            pl.BlockSpec((1, d), lambda i, j: (0, 0)),
            pl.BlockSpec((d, tn), lambda i, j: (0, j)),
            pl.BlockSpec((d, LR_PAD), lambda i, j: (0, 0)),
        ],
        out_specs=[
            pl.BlockSpec((tm, tn), lambda i, j: (i, j)),
            pl.BlockSpec((tm, LR_PAD), lambda i, j: (i, 0)),
        ],
        out_shape=[
            jax.ShapeDtypeStruct((m, MAIN_WIDTH), BF16),
            jax.ShapeDtypeStruct((m, LR_PAD), F32),
        ],
        scratch_shapes=[pltpu.VMEM((tm, d), BF16)],
        compiler_params=_cparams("parallel", "arbitrary"),
        name="inproj",
    )(x2, g1, w_main, w_lr)


def _norm_rope(t, gain, cos, sin):
    t = t * lax.rsqrt(jnp.mean(t * t, axis=-1, keepdims=True) + NORM_EPS) * gain
    return t * cos + pltpu.roll(t, HEAD_DIM // 2, 1) * sin


def _attn_kernel(sink_ref, q_ref, k_ref, v_ref, cos_ref, sin_ref, qg_ref, kg_ref, o_ref, kr_ref,
                 *, tq, seq):
    i = pl.program_id(1)

    @pl.when(i == 0)
    def _():
        def body(c, carry):
            r0 = pl.multiple_of(c * ROPE_ROWS, ROPE_ROWS)
            cs = cos_ref[pl.ds(r0, ROPE_ROWS), :]
            sn = sin_ref[pl.ds(r0, ROPE_ROWS), :]
            for h in range(KV_HEADS):
                cols = slice(h * HEAD_DIM, (h + 1) * HEAD_DIM)
                kk = k_ref[0, pl.ds(r0, ROPE_ROWS), cols].astype(F32)
                kr_ref[pl.ds(r0, ROPE_ROWS), cols] = _norm_rope(kk, kg_ref[...], cs, sn).astype(BF16)
            return carry

        lax.fori_loop(0, seq // ROPE_ROWS, body, 0)

    win = tq + 2 * WINDOW
    q0 = pl.multiple_of(i * tq, tq)
    start = pl.multiple_of(jnp.clip(q0 - WINDOW, 0, seq - win), WINDOW)
    cs = cos_ref[pl.ds(q0, tq), :]
    sn = sin_ref[pl.ds(q0, tq), :]
    row = lax.broadcasted_iota(jnp.int32, (tq, win), 0)
    col = lax.broadcasted_iota(jnp.int32, (tq, win), 1)
    mask = (jnp.abs((start + col) - (q0 + row)) <= WINDOW)[None]
    grp = lax.broadcasted_iota(jnp.int32, (GQA_GROUP, 1, 1), 0)
    scale = HEAD_DIM ** -0.5

    for kvh in range(KV_HEADS):
        cols = slice(kvh * HEAD_DIM, (kvh + 1) * HEAD_DIM)
        kw = kr_ref[pl.ds(start, win), cols]
        vw = v_ref[0, pl.ds(start, win), cols]
        qs = []
        sink = jnp.zeros((GQA_GROUP, 1, 1), F32)
        for g in range(GQA_GROUP):
            h = kvh * GQA_GROUP + g
            qq = q_ref[0, :, h * HEAD_DIM:(h + 1) * HEAD_DIM].astype(F32)
            qs.append((_norm_rope(qq, qg_ref[...], cs, sn) * scale).astype(BF16))
            sink = jnp.where(grp == g, sink_ref[h], sink)
        qst = jnp.concatenate(qs, axis=0)
        s = lax.dot_general(qst, kw, (((1,), (1,)), ((), ())), preferred_element_type=F32)
        s = jnp.where(mask, s.reshape(GQA_GROUP, tq, win), -jnp.inf)
        m = jnp.maximum(jnp.max(s, axis=-1, keepdims=True), sink)
        p = jnp.exp(s - m)
        denom = jnp.sum(p, axis=-1, keepdims=True) + jnp.exp(sink - m)
        o = jnp.dot(p.astype(BF16).reshape(GQA_GROUP * tq, win), vw, preferred_element_type=F32)
        o = o.reshape(GQA_GROUP, tq, HEAD_DIM) / denom
        for g in range(GQA_GROUP):
            h = kvh * GQA_GROUP + g
            o_ref[0, :, h * HEAD_DIM:(h + 1) * HEAD_DIM] = o[g].astype(o_ref.dtype)


def _attention(proj3, cos, sin, qg, kg, sink):
    b, seq, _ = proj3.shape
    tq = ATTN_TQ
    kvw = KV_HEADS * HEAD_DIM
    return pl.pallas_call(
        functools.partial(_attn_kernel, tq=tq, seq=seq),
        grid=(b, seq // tq),
        in_specs=[
            pl.BlockSpec(memory_space=pltpu.SMEM),
            pl.BlockSpec((1, tq, ATTN_WIDTH), lambda bi, i: (bi, i, OFF_QA // ATTN_WIDTH)),
            pl.BlockSpec((1, seq, kvw), lambda bi, i: (bi, 0, OFF_KA // kvw)),
            pl.BlockSpec((1, seq, kvw), lambda bi, i: (bi, 0, OFF_VA // kvw)),
            pl.BlockSpec((seq, HEAD_DIM), lambda bi, i: (0, 0)),
            pl.BlockSpec((seq, HEAD_DIM), lambda bi, i: (0, 0)),
            pl.BlockSpec((1, HEAD_DIM), lambda bi, i: (0, 0)),
            pl.BlockSpec((1, HEAD_DIM), lambda bi, i: (0, 0)),
        ],
        out_specs=pl.BlockSpec((1, tq, ATTN_WIDTH), lambda bi, i: (bi, i, 0)),
        out_shape=jax.ShapeDtypeStruct((b, seq, ATTN_WIDTH), BF16),
        scratch_shapes=[pltpu.VMEM((seq, kvw), BF16)],
        compiler_params=_cparams("parallel", "arbitrary"),
        name="swa",
    )(sink, proj3, proj3, proj3, cos, sin, qg, kg)


def _log_sigmoid(z):
    return -(jnp.maximum(-z, 0.0) + jnp.log1p(jnp.exp(-jnp.abs(z))))


def _split2(g):
    hi = g.astype(BF16)
    lo = (g - hi.astype(F32)).astype(BF16)
    return hi, lo


def _gla_kernel(q_ref, k_ref, v_ref, gate_ref, lr_ref, wa_ref, ba_ref, gn_ref,
                o_ref, qin_ref, kv_ref, dec_ref, acc_ref, st_ref, *, seq):
    c = GLA_CHUNK
    n = seq // c
    blk = GLA_BLOCK
    cpb = blk // c

    ri = lax.broadcasted_iota(jnp.int32, (blk, blk), 0)
    ci = lax.broadcasted_iota(jnp.int32, (blk, blk), 1)
    same = (ri // c) == (ci // c)
    lower_incl = same & (ci <= ri)
    upper_incl = same & (ci >= ri)
    upper_strict = same & (ci > ri)
    tri_f = jnp.where(lower_incl, 1.0, 0.0).astype(BF16)
    tri_b = jnp.where(upper_incl, 1.0, 0.0).astype(BF16)
    qscale = GLA_DK ** -0.5

    def intra(t, carry):
        r0 = pl.multiple_of(t * blk, blk)
        lr = lr_ref[0, pl.ds(r0, blk), :].astype(BF16)
        q3 = (q_ref[0, pl.ds(r0, blk), :].astype(F32) * qscale).reshape(cpb, c, GLA_DK)
        k3 = k_ref[0, pl.ds(r0, blk), :].astype(F32).reshape(cpb, c, GLA_DK)
        v = v_ref[0, pl.ds(r0, blk), :]

        z = jnp.dot(lr, wa_ref[...], preferred_element_type=F32) + ba_ref[...]
        g_hi, g_lo = _split2(_log_sigmoid(z) / GLA_GATE_NORMALIZER)

        def one_direction(d, tri, keep, ref_row, last_row, slot):
            cols = slice(d * GLA_DK, (d + 1) * GLA_DK)
            parts = jnp.concatenate([g_hi[:, cols], g_lo[:, cols]], axis=1)
            bsum = jnp.dot(tri, parts, preferred_element_type=F32)
            bcum = (bsum[:, :GLA_DK] + bsum[:, GLA_DK:]).reshape(cpb, c, GLA_DK)
            b_mid = bcum[:, ref_row:ref_row + 1, :]
            b_end = bcum[:, last_row:last_row + 1, :]
            qe = q3 * jnp.exp(bcum - b_mid)
            ke = k3 * jnp.exp(b_mid - bcum)
            qin_ref[slot, pl.ds(r0, blk), :] = (qe * jnp.exp(b_mid)).astype(BF16).reshape(blk, GLA_DK)
            kout = (ke * jnp.exp(b_end - b_mid)).astype(BF16)
            for j in range(cpb):
                kv_ref[slot, t * cpb + j] = lax.dot_general(
                    v[j * c:(j + 1) * c], kout[j], (((0,), (0,)), ((), ())), preferred_element_type=F32)
            dec = jnp.broadcast_to(jnp.exp(b_end), (cpb, 8, GLA_DK)).reshape(cpb * 8, GLA_DK)
            dec_ref[slot, pl.ds(pl.multiple_of(t * (cpb * 8), cpb * 8), cpb * 8), :] = dec
            a = lax.dot_general(qe.astype(BF16).reshape(blk, GLA_DK), ke.astype(BF16).reshape(blk, GLA_DK),
                                (((1,), (1,)), ((), ())), preferred_element_type=F32)
            a = jnp.where(keep, a, 0.0).astype(BF16)
            return jnp.dot(a, v, preferred_element_type=F32)

        o = one_direction(0, tri_f, lower_incl, c // 2, c - 1, 0)
        o = o + one_direction(1, tri_b, upper_strict, c - 1 - c // 2, 0, 1)
        acc_ref[pl.ds(r0, blk), :] = o
        return carry

    lax.fori_loop(0, seq // blk, intra, 0, unroll=2)

    def step(idx, slot):
        r0 = pl.multiple_of(idx * c, c)
        st = st_ref[slot]
        o = lax.dot_general(qin_ref[slot, pl.ds(r0, c), :], st.astype(BF16), (((1,), (1,)), ((), ())),
                            preferred_element_type=F32)
        dec = dec_ref[slot, pl.ds(pl.multiple_of(idx * 8, 8), 8), :]
        st_ref[slot] = ((st.reshape(GLA_DV // 8, 8, GLA_DK) * dec[None]).reshape(GLA_DV, GLA_DK)
                        + kv_ref[slot, idx])
        return r0, o

    def accumulate(r0, o):
        acc_ref[pl.ds(r0, c), :] += o

    def finalize(r0, o):
        o = o + acc_ref[pl.ds(r0, c), :]
        y = o * lax.rsqrt(jnp.mean(o * o, axis=-1, keepdims=True) + NORM_EPS) * gn_ref[...]
        gate = gate_ref[0, pl.ds(r0, c), :].astype(F32)
        o_ref[0, pl.ds(r0, c), :] = (y * (gate * jax.nn.sigmoid(gate))).astype(o_ref.dtype)

    st_ref[...] = jnp.zeros_like(st_ref)

    def first_half(t, carry):
        accumulate(*step(t, 0))
        accumulate(*step(n - 1 - t, 1))
        return carry

    def second_half(t, carry):
        finalize(*step(t, 0))
        finalize(*step(n - 1 - t, 1))
        return carry

    lax.fori_loop(0, n // 2, first_half, 0, unroll=4)
    lax.fori_loop(n // 2, n, second_half, 0, unroll=4)


def _gla(proj3, lr3, wa, ba, gn):
    b, seq, _ = proj3.shape
    assert (seq // GLA_CHUNK) % 8 == 0 and seq % GLA_BLOCK == 0
    return pl.pallas_call(
        functools.partial(_gla_kernel, seq=seq),
        grid=(b, GLA_HEADS),
        in_specs=[
            pl.BlockSpec((1, seq, GLA_DK), lambda bi, h: (bi, 0, OFF_QG // GLA_DK + h)),
            pl.BlockSpec((1, seq, GLA_DK), lambda bi, h: (bi, 0, OFF_KG // GLA_DK + h)),
            pl.BlockSpec((1, seq, GLA_DV), lambda bi, h: (bi, 0, OFF_VG // GLA_DV + h)),
            pl.BlockSpec((1, seq, GLA_DV), lambda bi, h: (bi, 0, OFF_GG // GLA_DV + h)),
            pl.BlockSpec((1, seq, LR_PAD), lambda bi, h: (bi, 0, 0)),
            pl.BlockSpec((LR_PAD, 2 * GLA_DK), lambda bi, h: (0, h)),
            pl.BlockSpec((1, 2 * GLA_DK), lambda bi, h: (0, h)),
            pl.BlockSpec((1, GLA_DV), lambda bi, h: (0, 0)),
        ],
        out_specs=pl.BlockSpec((1, seq, GLA_DV), lambda bi, h: (bi, 0, h)),
        out_shape=jax.ShapeDtypeStruct((b, seq, GLA_WIDTH), BF16),
        scratch_shapes=[
            pltpu.VMEM((2, seq, GLA_DK), BF16),
            pltpu.VMEM((2, seq // GLA_CHUNK, GLA_DV, GLA_DK), F32),
            pltpu.VMEM((2, seq // GLA_CHUNK * 8, GLA_DK), F32),
            pltpu.VMEM((seq, GLA_DV), F32),
            pltpu.VMEM((2, GLA_DV, GLA_DK), F32),
        ],
        compiler_params=_cparams("parallel", "arbitrary"),
        name="gla",
    )(proj3, proj3, proj3, proj3, lr3, wa, ba, gn)


def _outproj_kernel(oa_ref, og_ref, x_ref, w_ref, g_ref, x1_ref, h2_ref):
    wa = oa_ref.shape[1]
    acc = jnp.dot(oa_ref[...], w_ref[:wa, :], preferred_element_type=F32)
    acc = acc + jnp.dot(og_ref[...], w_ref[wa:, :], preferred_element_type=F32)
    x1 = x_ref[...] + acc
    x1_ref[...] = x1
    ms = jnp.mean(x1 * x1, axis=-1, keepdims=True)
    h2_ref[...] = (x1 * lax.rsqrt(ms + NORM_EPS) * g_ref[...]).astype(h2_ref.dtype)


def _outproj(oa, og, x2, w_out, g2):
    m, d = x2.shape
    tm = OUTPROJ_TM
    wa, wg = oa.shape[1], og.shape[1]
    return pl.pallas_call(
        _outproj_kernel,
        grid=(m // tm,),
        in_specs=[
            pl.BlockSpec((tm, wa), lambda i: (i, 0)),
            pl.BlockSpec((tm, wg), lambda i: (i, 0)),
            pl.BlockSpec((tm, d), lambda i: (i, 0)),
            pl.BlockSpec((wa + wg, d), lambda i: (0, 0)),
            pl.BlockSpec((1, d), lambda i: (0, 0)),
        ],
        out_specs=[
            pl.BlockSpec((tm, d), lambda i: (i, 0)),
            pl.BlockSpec((tm, d), lambda i: (i, 0)),
        ],
        out_shape=[
            jax.ShapeDtypeStruct((m, d), F32),
            jax.ShapeDtypeStruct((m, d), BF16),
        ],
        compiler_params=_cparams("parallel"),
        name="outproj",
    )(oa, og, x2, w_out, g2)


def _ffn_kernel(h_ref, hp_ref, hn_ref, x1_ref, wg_ref, wv_ref, cwg_ref, cwv_ref, cbg_ref, cbv_ref,
                wd_ref, o_ref, hext_ref, acc_ref, *, tm, tiles_per_seq):
    i = pl.program_id(0)
    f = pl.program_id(1)

    @pl.when(f == 0)
    def _():
        hext_ref[:tm, :] = h_ref[...]
        pos = i % tiles_per_seq
        r = lax.broadcasted_iota(jnp.int32, (HALO, 1), 0)
        take_next = (r == 0) & (pos != tiles_per_seq - 1)
        take_prev = (r == HALO - 1) & (pos != 0)
        zero = jnp.zeros_like(hn_ref[...])
        hext_ref[tm:, :] = jnp.where(take_next, hn_ref[...], jnp.where(take_prev, hp_ref[...], zero))
        acc_ref[...] = jnp.zeros_like(acc_ref)

    hext = hext_ref[...]
    rows = tm + HALO

    def conv(u, cw_ref, cb_ref):
        prev = pltpu.roll(u, 1, 0)
        nxt = pltpu.roll(u, rows - 1, 0)
        y = prev * cw_ref[0:1, :] + u * cw_ref[1:2, :] + nxt * cw_ref[2:3, :] + cb_ref[...]
        return y[:tm]

    ug = conv(jnp.dot(hext, wg_ref[...], preferred_element_type=F32), cwg_ref, cbg_ref)
    uv = conv(jnp.dot(hext, wv_ref[...], preferred_element_type=F32), cwv_ref, cbv_ref)
    act = (ug * jax.nn.sigmoid(ug) * uv).astype(BF16)
    acc_ref[...] += jnp.dot(act, wd_ref[...], preferred_element_type=F32)

    @pl.when(f == pl.num_programs(1) - 1)
    def _():
        o_ref[...] = x1_ref[...] + acc_ref[...]


def _ffn(h2, x1, w_up, conv_w, conv_b, w_down, seq):
    m, d = h2.shape
    dff = w_down.shape[0]
    tm, tf = FFN_TM, FFN_TF
    nf = dff // tf
    hb = tm // HALO
    last_halo = m // HALO - 1
    return pl.pallas_call(
        functools.partial(_ffn_kernel, tm=tm, tiles_per_seq=seq // tm),
        grid=(m // tm, nf),
        in_specs=[
            pl.BlockSpec((tm, d), lambda i, f: (i, 0)),
            pl.BlockSpec((HALO, d), lambda i, f: (jnp.maximum(i * hb - 1, 0), 0)),
            pl.BlockSpec((HALO, d), lambda i, f: (jnp.minimum((i + 1) * hb, last_halo), 0)),
            pl.BlockSpec((tm, d), lambda i, f: (i, 0)),
            pl.BlockSpec((d, tf), lambda i, f: (0, f)),
            pl.BlockSpec((d, tf), lambda i, f: (0, nf + f)),
            pl.BlockSpec((3, tf), lambda i, f: (0, f)),
            pl.BlockSpec((3, tf), lambda i, f: (0, nf + f)),
            pl.BlockSpec((1, tf), lambda i, f: (0, f)),
            pl.BlockSpec((1, tf), lambda i, f: (0, nf + f)),
            pl.BlockSpec((tf, d), lambda i, f: (f, 0)),
        ],
        out_specs=pl.BlockSpec((tm, d), lambda i, f: (i, 0)),
        out_shape=jax.ShapeDtypeStruct((m, d), F32),
        scratch_shapes=[
            pltpu.VMEM((tm + HALO, d), BF16),
            pltpu.VMEM((tm, d), F32),
        ],
        compiler_params=_cparams("parallel", "arbitrary"),
        name="convffn",
    )(h2, h2, h2, x1, w_up, w_up, conv_w, conv_w, conv_b, conv_b, w_down)


def _rope_tables(seq):
    half = HEAD_DIM // 2
    inv = 1.0 / (ROPE_THETA ** (jnp.arange(half, dtype=F32) / half))
    ang = jnp.arange(seq, dtype=F32)[:, None] * inv[None, :]
    cos = jnp.cos(ang)
    sin = jnp.sin(ang)
    return jnp.concatenate([cos, cos], axis=-1), jnp.concatenate([-sin, sin], axis=-1)


def _decay_params(wa2_f, ba_f, wa2_b, ba_b):
    r = GLA_DECAY_RANK
    wf = jnp.pad(wa2_f, ((0, LR_PAD - r), (0, 0))).reshape(LR_PAD, GLA_HEADS, 1, GLA_DK)
    wb = jnp.pad(wa2_b, ((r, LR_PAD - 2 * r), (0, 0))).reshape(LR_PAD, GLA_HEADS, 1, GLA_DK)
    wa = jnp.concatenate([wf, wb], axis=2).reshape(LR_PAD, 2 * GLA_KEY_WIDTH).astype(BF16)
    ba = jnp.concatenate([ba_f.reshape(GLA_HEADS, 1, GLA_DK), ba_b.reshape(GLA_HEADS, 1, GLA_DK)],
                         axis=1).reshape(1, 2 * GLA_KEY_WIDTH)
    return wa, ba


def _layer(x, norm1_g, w_in, q_norm_g, k_norm_g, sink, wa2_f, ba_f, wa2_b, ba_b, gla_norm_g,
           w_out, norm2_g, w_up, conv_w, conv_b, w_down):
    b, seq, d = x.shape
    m = b * seq
    x2 = x.reshape(m, d)
    w_main = w_in[:, :MAIN_WIDTH].astype(BF16)
    w_lr = jnp.pad(w_in[:, MAIN_WIDTH:], ((0, 0), (0, LR_PAD - 2 * GLA_DECAY_RANK))).astype(BF16)
    proj, lr = _inproj(x2, norm1_g.reshape(1, d), w_main, w_lr)
    proj3 = proj.reshape(b, seq, MAIN_WIDTH)
    cos, sin = _rope_tables(seq)
    o_attn = _attention(proj3, cos, sin, q_norm_g.reshape(1, HEAD_DIM), k_norm_g.reshape(1, HEAD_DIM), sink)
    wa, ba = _decay_params(wa2_f, ba_f, wa2_b, ba_b)
    o_gla = _gla(proj3, lr.reshape(b, seq, LR_PAD), wa, ba, gla_norm_g.reshape(1, GLA_DV))
    x1, h2 = _outproj(o_attn.reshape(m, ATTN_WIDTH), o_gla.reshape(m, GLA_WIDTH), x2,
                      w_out.astype(BF16), norm2_g.reshape(1, d))
    out = _ffn(h2, x1, w_up.astype(BF16), conv_w, conv_b.reshape(1, -1), w_down.astype(BF16), seq)
    return out.reshape(b, seq, d)


def kernel(x, norm1_g, w_in, attn_q_norm_g, attn_k_norm_g, attn_sink, gla_wa2_fwd, gla_ba_fwd, gla_wa2_bwd, gla_ba_bwd, gla_out_norm_g, w_out, norm2_g, w_up, conv_w, conv_b, w_down):
    for l in range(w_in.shape[0]):
        x = _layer(x, norm1_g[l], w_in[l], attn_q_norm_g[l], attn_k_norm_g[l], attn_sink[l],
                   gla_wa2_fwd[l], gla_ba_fwd[l], gla_wa2_bwd[l], gla_ba_bwd[l], gla_out_norm_g[l],
                   w_out[l], norm2_g[l], w_up[l], conv_w[l], conv_b[l], w_down[l])
    return x
```

```python
import functools

import jax
import jax.numpy as jnp
from jax import lax
from jax.experimental import pallas as pl
from jax.experimental.pallas import tpu as pltpu

F32 = jnp.float32
BF16 = jnp.bfloat16

HEAD_DIM = 128
KV_HEADS = 2
GQA_GROUP = 4
ATTN_HEADS = KV_HEADS * GQA_GROUP
ATTN_WIDTH = ATTN_HEADS * HEAD_DIM
WINDOW = 128
ROPE_THETA = 10000.0
GLA_HEADS = 4
GLA_DK = 128
GLA_DV = 256
GLA_KEY_WIDTH = GLA_HEADS * GLA_DK
GLA_WIDTH = GLA_HEADS * GLA_DV
GLA_DECAY_RANK = 16
GLA_GATE_NORMALIZER = 16.0
GLA_CHUNK = 64
GLA_BLOCK = 256
NORM_EPS = 1e-6

OFF_QA = 0
OFF_KA = OFF_QA + ATTN_WIDTH
OFF_VA = OFF_KA + KV_HEADS * HEAD_DIM
OFF_QG = OFF_VA + KV_HEADS * HEAD_DIM
OFF_KG = OFF_QG + GLA_KEY_WIDTH
OFF_VG = OFF_KG + GLA_KEY_WIDTH
OFF_GG = OFF_VG + GLA_WIDTH
OFF_LR = OFF_GG + GLA_WIDTH
MAIN_WIDTH = OFF_LR
LR_PAD = 128

VMEM_LIMIT_BYTES = 56 * 1024 * 1024
INPROJ_TM = 1024
INPROJ_TN = 1536
ATTN_TQ = 128
ROPE_ROWS = 512
OUTPROJ_TM = 512
FFN_TM = 512
FFN_TF = 512
HALO = 16


def _cparams(*sem):
    return pltpu.CompilerParams(dimension_semantics=sem, vmem_limit_bytes=VMEM_LIMIT_BYTES)


def _inproj_kernel(x_ref, g_ref, w_ref, wlr_ref, o_ref, lr_ref, h_ref):
    @pl.when(pl.program_id(1) == 0)
    def _():
        x = x_ref[...]
        ms = jnp.mean(x * x, axis=-1, keepdims=True)
        h = (x * lax.rsqrt(ms + NORM_EPS) * g_ref[...]).astype(BF16)
        h_ref[...] = h
        lr_ref[...] = jnp.dot(h, wlr_ref[...], preferred_element_type=F32)

    o_ref[...] = jnp.dot(h_ref[...], w_ref[...], preferred_element_type=F32).astype(o_ref.dtype)


def _inproj(x2, g1, w_main, w_lr):
    m, d = x2.shape
    tm, tn = INPROJ_TM, INPROJ_TN
    return pl.pallas_call(
        _inproj_kernel,
        grid=(m // tm, MAIN_WIDTH // tn),
        in_specs=[
            pl.BlockSpec((tm, d), lambda i, j: (i, 0)),
            pl.BlockSpec((1, d), lambda i, j: (0, 0)),
            pl.BlockSpec((d, tn), lambda i, j: (0, j)),
            pl.BlockSpec((d, LR_PAD), lambda i, j: (0, 0)),
        ],
        out_specs=[
            pl.BlockSpec((tm, tn), lambda i, j: (i, j)),
            pl.BlockSpec((tm, LR_PAD), lambda i, j: (i, 0)),
        ],
        out_shape=[
            jax.ShapeDtypeStruct((m, MAIN_WIDTH), BF16),
            jax.ShapeDtypeStruct((m, LR_PAD), F32),
        ],
        scratch_shapes=[pltpu.VMEM((tm, d), BF16)],
        compiler_params=_cparams("parallel", "arbitrary"),
        name="inproj",
    )(x2, g1, w_main, w_lr)


def _norm_rope(t, gain, cos, sin):
    t = t * lax.rsqrt(jnp.mean(t * t, axis=-1, keepdims=True) + NORM_EPS) * gain
    return t * cos + pltpu.roll(t, HEAD_DIM // 2, 1) * sin


def _attn_kernel(sink_ref, q_ref, k_ref, v_ref, cos_ref, sin_ref, qg_ref, kg_ref, o_ref, kr_ref,
                 *, tq, seq):
    i = pl.program_id(1)

    @pl.when(i == 0)
    def _():
        def body(c, carry):
            r0 = pl.multiple_of(c * ROPE_ROWS, ROPE_ROWS)
            cs = cos_ref[pl.ds(r0, ROPE_ROWS), :]
            sn = sin_ref[pl.ds(r0, ROPE_ROWS), :]
            for h in range(KV_HEADS):
                cols = slice(h * HEAD_DIM, (h + 1) * HEAD_DIM)
                kk = k_ref[0, pl.ds(r0, ROPE_ROWS), cols].astype(F32)
                kr_ref[pl.ds(r0, ROPE_ROWS), cols] = _norm_rope(kk, kg_ref[...], cs, sn).astype(BF16)
            return carry

        lax.fori_loop(0, seq // ROPE_ROWS, body, 0)

    win = tq + 2 * WINDOW
    q0 = pl.multiple_of(i * tq, tq)
    start = pl.multiple_of(jnp.clip(q0 - WINDOW, 0, seq - win), WINDOW)
    cs = cos_ref[pl.ds(q0, tq), :]
    sn = sin_ref[pl.ds(q0, tq), :]
    row = lax.broadcasted_iota(jnp.int32, (tq, win), 0)
    col = lax.broadcasted_iota(jnp.int32, (tq, win), 1)
    mask = (jnp.abs((start + col) - (q0 + row)) <= WINDOW)[None]
    grp = lax.broadcasted_iota(jnp.int32, (GQA_GROUP, 1, 1), 0)
    scale = HEAD_DIM ** -0.5

    for kvh in range(KV_HEADS):
        cols = slice(kvh * HEAD_DIM, (kvh + 1) * HEAD_DIM)
        kw = kr_ref[pl.ds(start, win), cols]
        vw = v_ref[0, pl.ds(start, win), cols]
        qs = []
        sink = jnp.zeros((GQA_GROUP, 1, 1), F32)
        for g in range(GQA_GROUP):
            h = kvh * GQA_GROUP + g
            qq = q_ref[0, :, h * HEAD_DIM:(h + 1) * HEAD_DIM].astype(F32)
            qs.append((_norm_rope(qq, qg_ref[...], cs, sn) * scale).astype(BF16))
            sink = jnp.where(grp == g, sink_ref[h], sink)
        qst = jnp.concatenate(qs, axis=0)
        s = lax.dot_general(qst, kw, (((1,), (1,)), ((), ())), preferred_element_type=F32)
        s = jnp.where(mask, s.reshape(GQA_GROUP, tq, win), -jnp.inf)
        m = jnp.maximum(jnp.max(s, axis=-1, keepdims=True), sink)
        p = jnp.exp(s - m)
        denom = jnp.sum(p, axis=-1, keepdims=True) + jnp.exp(sink - m)
        o = jnp.dot(p.astype(BF16).reshape(GQA_GROUP * tq, win), vw, preferred_element_type=F32)
        o = o.reshape(GQA_GROUP, tq, HEAD_DIM) / denom
        for g in range(GQA_GROUP):
            h = kvh * GQA_GROUP + g
            o_ref[0, :, h * HEAD_DIM:(h + 1) * HEAD_DIM] = o[g].astype(o_ref.dtype)


def _attention(proj3, cos, sin, qg, kg, sink):
    b, seq, _ = proj3.shape
    tq = ATTN_TQ
    kvw = KV_HEADS * HEAD_DIM
    return pl.pallas_call(
        functools.partial(_attn_kernel, tq=tq, seq=seq),
        grid=(b, seq // tq),
        in_specs=[
            pl.BlockSpec(memory_space=pltpu.SMEM),
            pl.BlockSpec((1, tq, ATTN_WIDTH), lambda bi, i: (bi, i, OFF_QA // ATTN_WIDTH)),
            pl.BlockSpec((1, seq, kvw), lambda bi, i: (bi, 0, OFF_KA // kvw)),
            pl.BlockSpec((1, seq, kvw), lambda bi, i: (bi, 0, OFF_VA // kvw)),
            pl.BlockSpec((seq, HEAD_DIM), lambda bi, i: (0, 0)),
            pl.BlockSpec((seq, HEAD_DIM), lambda bi, i: (0, 0)),
            pl.BlockSpec((1, HEAD_DIM), lambda bi, i: (0, 0)),
            pl.BlockSpec((1, HEAD_DIM), lambda bi, i: (0, 0)),
        ],
        out_specs=pl.BlockSpec((1, tq, ATTN_WIDTH), lambda bi, i: (bi, i, 0)),
        out_shape=jax.ShapeDtypeStruct((b, seq, ATTN_WIDTH), BF16),
        scratch_shapes=[pltpu.VMEM((seq, kvw), BF16)],
        compiler_params=_cparams("parallel", "arbitrary"),
        name="swa",
    )(sink, proj3, proj3, proj3, cos, sin, qg, kg)


def _log_sigmoid(z):
    return -(jnp.maximum(-z, 0.0) + jnp.log1p(jnp.exp(-jnp.abs(z))))


def _split2(g):
    hi = g.astype(BF16)
    lo = (g - hi.astype(F32)).astype(BF16)
    return hi, lo


def _gla_kernel(q_ref, k_ref, v_ref, gate_ref, lr_ref, wa_ref, ba_ref, gn_ref,
                o_ref, qin_ref, kv_ref, dec_ref, acc_ref, st_ref, *, seq):
    c = GLA_CHUNK
    n = seq // c
    blk = GLA_BLOCK
    cpb = blk // c

    ri = lax.broadcasted_iota(jnp.int32, (blk, blk), 0)
    ci = lax.broadcasted_iota(jnp.int32, (blk, blk), 1)
    same = (ri // c) == (ci // c)
    lower_incl = same & (ci <= ri)
    upper_incl = same & (ci >= ri)
    upper_strict = same & (ci > ri)
    tri_f = jnp.where(lower_incl, 1.0, 0.0).astype(BF16)
    tri_b = jnp.where(upper_incl, 1.0, 0.0).astype(BF16)
    qscale = GLA_DK ** -0.5

    def intra(t, carry):
        r0 = pl.multiple_of(t * blk, blk)
        lr = lr_ref[0, pl.ds(r0, blk), :].astype(BF16)
        q3 = (q_ref[0, pl.ds(r0, blk), :].astype(F32) * qscale).reshape(cpb, c, GLA_DK)
        k3 = k_ref[0, pl.ds(r0, blk), :].astype(F32).reshape(cpb, c, GLA_DK)
        v = v_ref[0, pl.ds(r0, blk), :]

        z = jnp.dot(lr, wa_ref[...], preferred_element_type=F32) + ba_ref[...]
        g_hi, g_lo = _split2(_log_sigmoid(z) / GLA_GATE_NORMALIZER)

        def one_direction(d, tri, keep, ref_row, last_row, slot):
            cols = slice(d * GLA_DK, (d + 1) * GLA_DK)
            parts = jnp.concatenate([g_hi[:, cols], g_lo[:, cols]], axis=1)
            bsum = jnp.dot(tri, parts, preferred_element_type=F32)
            bcum = (bsum[:, :GLA_DK] + bsum[:, GLA_DK:]).reshape(cpb, c, GLA_DK)
            b_mid = bcum[:, ref_row:ref_row + 1, :]
            b_end = bcum[:, last_row:last_row + 1, :]
            qe = q3 * jnp.exp(bcum - b_mid)
            ke = k3 * jnp.exp(b_mid - bcum)
            qin_ref[slot, pl.ds(r0, blk), :] = (qe * jnp.exp(b_mid)).astype(BF16).reshape(blk, GLA_DK)
            kout = (ke * jnp.exp(b_end - b_mid)).astype(BF16)
            for j in range(cpb):
                kv_ref[slot, t * cpb + j] = lax.dot_general(
                    v[j * c:(j + 1) * c], kout[j], (((0,), (0,)), ((), ())), preferred_element_type=F32)
            dec = jnp.broadcast_to(jnp.exp(b_end), (cpb, 8, GLA_DK)).reshape(cpb * 8, GLA_DK)
            dec_ref[slot, pl.ds(pl.multiple_of(t * (cpb * 8), cpb * 8), cpb * 8), :] = dec
            a = lax.dot_general(qe.astype(BF16).reshape(blk, GLA_DK), ke.astype(BF16).reshape(blk, GLA_DK),
                                (((1,), (1,)), ((), ())), preferred_element_type=F32)
            a = jnp.where(keep, a, 0.0).astype(BF16)
            return jnp.dot(a, v, preferred_element_type=F32)

        o = one_direction(0, tri_f, lower_incl, c // 2, c - 1, 0)
        o = o + one_direction(1, tri_b, upper_strict, c - 1 - c // 2, 0, 1)
        acc_ref[pl.ds(r0, blk), :] = o
        return carry

    lax.fori_loop(0, seq // blk, intra, 0, unroll=4)

    def step(idx, slot):
        r0 = pl.multiple_of(idx * c, c)
        st = st_ref[slot]
        o = lax.dot_general(qin_ref[slot, pl.ds(r0, c), :], st.astype(BF16), (((1,), (1,)), ((), ())),
                            preferred_element_type=F32)
        dec = dec_ref[slot, pl.ds(pl.multiple_of(idx * 8, 8), 8), :]
        st_ref[slot] = ((st.reshape(GLA_DV // 8, 8, GLA_DK) * dec[None]).reshape(GLA_DV, GLA_DK)
                        + kv_ref[slot, idx])
        return r0, o

    def accumulate(r0, o):
        acc_ref[pl.ds(r0, c), :] += o

    def finalize(r0, o):
        o = o + acc_ref[pl.ds(r0, c), :]
        y = o * lax.rsqrt(jnp.mean(o * o, axis=-1, keepdims=True) + NORM_EPS) * gn_ref[...]
        gate = gate_ref[0, pl.ds(r0, c), :].astype(F32)
        o_ref[0, pl.ds(r0, c), :] = (y * (gate * jax.nn.sigmoid(gate))).astype(o_ref.dtype)

    st_ref[...] = jnp.zeros_like(st_ref)

    def first_half(t, carry):
        accumulate(*step(t, 0))
        accumulate(*step(n - 1 - t, 1))
        return carry

    def second_half(t, carry):
        finalize(*step(t, 0))
        finalize(*step(n - 1 - t, 1))
        return carry

    lax.fori_loop(0, n // 2, first_half, 0, unroll=4)
    lax.fori_loop(n // 2, n, second_half, 0, unroll=4)


def _gla(proj3, lr3, wa, ba, gn):
    b, seq, _ = proj3.shape
    assert (seq // GLA_CHUNK) % 8 == 0 and seq % GLA_BLOCK == 0
    return pl.pallas_call(
        functools.partial(_gla_kernel, seq=seq),
        grid=(b, GLA_HEADS),
        in_specs=[
            pl.BlockSpec((1, seq, GLA_DK), lambda bi, h: (bi, 0, OFF_QG // GLA_DK + h)),
            pl.BlockSpec((1, seq, GLA_DK), lambda bi, h: (bi, 0, OFF_KG // GLA_DK + h)),
            pl.BlockSpec((1, seq, GLA_DV), lambda bi, h: (bi, 0, OFF_VG // GLA_DV + h)),
            pl.BlockSpec((1, seq, GLA_DV), lambda bi, h: (bi, 0, OFF_GG // GLA_DV + h)),
            pl.BlockSpec((1, seq, LR_PAD), lambda bi, h: (bi, 0, 0)),
            pl.BlockSpec((LR_PAD, 2 * GLA_DK), lambda bi, h: (0, h)),
            pl.BlockSpec((1, 2 * GLA_DK), lambda bi, h: (0, h)),
            pl.BlockSpec((1, GLA_DV), lambda bi, h: (0, 0)),
        ],
        out_specs=pl.BlockSpec((1, seq, GLA_DV), lambda bi, h: (bi, 0, h)),
        out_shape=jax.ShapeDtypeStruct((b, seq, GLA_WIDTH), BF16),
        scratch_shapes=[
            pltpu.VMEM((2, seq, GLA_DK), BF16),
            pltpu.VMEM((2, seq // GLA_CHUNK, GLA_DV, GLA_DK), F32),
            pltpu.VMEM((2, seq // GLA_CHUNK * 8, GLA_DK), F32),
            pltpu.VMEM((seq, GLA_DV), F32),
            pltpu.VMEM((2, GLA_DV, GLA_DK), F32),
        ],
        compiler_params=_cparams("parallel", "arbitrary"),
        name="gla",
    )(proj3, proj3, proj3, proj3, lr3, wa, ba, gn)


def _outproj_kernel(oa_ref, og_ref, x_ref, w_ref, g_ref, x1_ref, h2_ref):
    wa = oa_ref.shape[1]
    acc = jnp.dot(oa_ref[...], w_ref[:wa, :], preferred_element_type=F32)
    acc = acc + jnp.dot(og_ref[...], w_ref[wa:, :], preferred_element_type=F32)
    x1 = x_ref[...] + acc
    x1_ref[...] = x1
    ms = jnp.mean(x1 * x1, axis=-1, keepdims=True)
    h2_ref[...] = (x1 * lax.rsqrt(ms + NORM_EPS) * g_ref[...]).astype(h2_ref.dtype)


def _outproj(oa, og, x2, w_out, g2):
    m, d = x2.shape
    tm = OUTPROJ_TM
    wa, wg = oa.shape[1], og.shape[1]
    return pl.pallas_call(
        _outproj_kernel,
        grid=(m // tm,),
        in_specs=[
            pl.BlockSpec((tm, wa), lambda i: (i, 0)),
            pl.BlockSpec((tm, wg), lambda i: (i, 0)),
            pl.BlockSpec((tm, d), lambda i: (i, 0)),
            pl.BlockSpec((wa + wg, d), lambda i: (0, 0)),
            pl.BlockSpec((1, d), lambda i: (0, 0)),
        ],
        out_specs=[
            pl.BlockSpec((tm, d), lambda i: (i, 0)),
            pl.BlockSpec((tm, d), lambda i: (i, 0)),
        ],
        out_shape=[
            jax.ShapeDtypeStruct((m, d), F32),
            jax.ShapeDtypeStruct((m, d), BF16),
        ],
        compiler_params=_cparams("parallel"),
        name="outproj",
    )(oa, og, x2, w_out, g2)


def _ffn_kernel(h_ref, hp_ref, hn_ref, x1_ref, wg_ref, wv_ref, cwg_ref, cwv_ref, cbg_ref, cbv_ref,
                wd_ref, o_ref, hext_ref, acc_ref, *, tm, tiles_per_seq):
    i = pl.program_id(0)
    f = pl.program_id(1)

    th = tm // 2
    rows = th + HALO

    @pl.when(f == 0)
    def _():
        pos = i % tiles_per_seq
        r = lax.broadcasted_iota(jnp.int32, (HALO, 1), 0)
        zero = jnp.zeros_like(hn_ref[...])
        prev0 = jnp.where((r == HALO - 1) & (pos != 0), hp_ref[...], zero)
        next1 = jnp.where((r == 0) & (pos != tiles_per_seq - 1), hn_ref[...], zero)
        hext_ref[0, :th, :] = h_ref[:th, :]
        hext_ref[0, th:, :] = jnp.where(r == 0, h_ref[th:th + HALO, :], prev0)
        hext_ref[1, :th, :] = h_ref[th:, :]
        hext_ref[1, th:, :] = jnp.where(r == HALO - 1, h_ref[th - HALO:th, :], next1)
        acc_ref[...] = jnp.zeros_like(acc_ref)

    def conv(u, cw_ref, cb_ref):
        prev = pltpu.roll(u, 1, 0)
        nxt = pltpu.roll(u, rows - 1, 0)
        y = prev * cw_ref[0:1, :] + u * cw_ref[1:2, :] + nxt * cw_ref[2:3, :] + cb_ref[...]
        return y[:th]

    acts = []
    for half in range(2):
        hext = hext_ref[half]
        ug = conv(jnp.dot(hext, wg_ref[...], preferred_element_type=F32), cwg_ref, cbg_ref)
        uv = conv(jnp.dot(hext, wv_ref[...], preferred_element_type=F32), cwv_ref, cbv_ref)
        acts.append((ug * jax.nn.sigmoid(ug) * uv).astype(BF16))
    for half in range(2):
        acc_ref[half] += jnp.dot(acts[half], wd_ref[...], preferred_element_type=F32)

    @pl.when(f == pl.num_programs(1) - 1)
    def _():
        o_ref[:th, :] = x1_ref[:th, :] + acc_ref[0]
        o_ref[th:, :] = x1_ref[th:, :] + acc_ref[1]


def _ffn(h2, x1, w_up, conv_w, conv_b, w_down, seq):
    m, d = h2.shape
    dff = w_down.shape[0]
    tm, tf = FFN_TM, FFN_TF
    nf = dff // tf
    hb = tm // HALO
    last_halo = m // HALO - 1
    return pl.pallas_call(
        functools.partial(_ffn_kernel, tm=tm, tiles_per_seq=seq // tm),
        grid=(m // tm, nf),
        in_specs=[
            pl.BlockSpec((tm, d), lambda i, f: (i, 0)),
            pl.BlockSpec((HALO, d), lambda i, f: (jnp.maximum(i * hb - 1, 0), 0)),
            pl.BlockSpec((HALO, d), lambda i, f: (jnp.minimum((i + 1) * hb, last_halo), 0)),
            pl.BlockSpec((tm, d), lambda i, f: (i, 0)),
            pl.BlockSpec((d, tf), lambda i, f: (0, f)),
            pl.BlockSpec((d, tf), lambda i, f: (0, nf + f)),
            pl.BlockSpec((3, tf), lambda i, f: (0, f)),
            pl.BlockSpec((3, tf), lambda i, f: (0, nf + f)),
            pl.BlockSpec((1, tf), lambda i, f: (0, f)),
            pl.BlockSpec((1, tf), lambda i, f: (0, nf + f)),
            pl.BlockSpec((tf, d), lambda i, f: (f, 0)),
        ],
        out_specs=pl.BlockSpec((tm, d), lambda i, f: (i, 0)),
        out_shape=jax.ShapeDtypeStruct((m, d), F32),
        scratch_shapes=[
            pltpu.VMEM((2, tm // 2 + HALO, d), BF16),
            pltpu.VMEM((2, tm // 2, d), F32),
        ],
        compiler_params=_cparams("parallel", "arbitrary"),
        name="convffn",
    )(h2, h2, h2, x1, w_up, w_up, conv_w, conv_w, conv_b, conv_b, w_down)


def _rope_tables(seq):
    half = HEAD_DIM // 2
    inv = 1.0 / (ROPE_THETA ** (jnp.arange(half, dtype=F32) / half))
    ang = jnp.arange(seq, dtype=F32)[:, None] * inv[None, :]
    cos = jnp.cos(ang)
    sin = jnp.sin(ang)
    return jnp.concatenate([cos, cos], axis=-1), jnp.concatenate([-sin, sin], axis=-1)


def _decay_params(wa2_f, ba_f, wa2_b, ba_b):
    r = GLA_DECAY_RANK
    wf = jnp.pad(wa2_f, ((0, LR_PAD - r), (0, 0))).reshape(LR_PAD, GLA_HEADS, 1, GLA_DK)
    wb = jnp.pad(wa2_b, ((r, LR_PAD - 2 * r), (0, 0))).reshape(LR_PAD, GLA_HEADS, 1, GLA_DK)
    wa = jnp.concatenate([wf, wb], axis=2).reshape(LR_PAD, 2 * GLA_KEY_WIDTH).astype(BF16)
    ba = jnp.concatenate([ba_f.reshape(GLA_HEADS, 1, GLA_DK), ba_b.reshape(GLA_HEADS, 1, GLA_DK)],
                         axis=1).reshape(1, 2 * GLA_KEY_WIDTH)
    return wa, ba


def _layer(x, norm1_g, w_in, q_norm_g, k_norm_g, sink, wa2_f, ba_f, wa2_b, ba_b, gla_norm_g,
           w_out, norm2_g, w_up, conv_w, conv_b, w_down):
    b, seq, d = x.shape
    m = b * seq
    x2 = x.reshape(m, d)
    w_lr = jnp.pad(w_in[:, MAIN_WIDTH:], ((0, 0), (0, LR_PAD - 2 * GLA_DECAY_RANK))).astype(BF16)
    proj, lr = _inproj(x2, norm1_g.reshape(1, d), w_in.astype(BF16), w_lr)
    proj3 = proj.reshape(b, seq, MAIN_WIDTH)
    cos, sin = _rope_tables(seq)
    o_attn = _attention(proj3, cos, sin, q_norm_g.reshape(1, HEAD_DIM), k_norm_g.reshape(1, HEAD_DIM), sink)
    wa, ba = _decay_params(wa2_f, ba_f, wa2_b, ba_b)
    o_gla = _gla(proj3, lr.reshape(b, seq, LR_PAD), wa, ba, gla_norm_g.reshape(1, GLA_DV))
    x1, h2 = _outproj(o_attn.reshape(m, ATTN_WIDTH), o_gla.reshape(m, GLA_WIDTH), x2,
                      w_out.astype(BF16), norm2_g.reshape(1, d))
    out = _ffn(h2, x1, w_up.astype(BF16), conv_w, conv_b.reshape(1, -1), w_down.astype(BF16), seq)
    return out.reshape(b, seq, d)


def kernel(x, norm1_g, w_in, attn_q_norm_g, attn_k_norm_g, attn_sink, gla_wa2_fwd, gla_ba_fwd, gla_wa2_bwd, gla_ba_bwd, gla_out_norm_g, w_out, norm2_g, w_up, conv_w, conv_b, w_down):
    for l in range(w_in.shape[0]):
        x = _layer(x, norm1_g[l], w_in[l], attn_q_norm_g[l], attn_k_norm_g[l], attn_sink[l],
                   gla_wa2_fwd[l], gla_ba_fwd[l], gla_wa2_bwd[l], gla_ba_bwd[l], gla_out_norm_g[l],
                   w_out[l], norm2_g[l], w_up[l], conv_w[l], conv_b[l], w_down[l])
    return x
```

```python
import functools

import jax
import jax.numpy as jnp
from jax import lax
from jax.experimental import pallas as pl
from jax.experimental.pallas import tpu as pltpu

F32 = jnp.float32
BF16 = jnp.bfloat16

HEAD_DIM = 128
KV_HEADS = 2
GQA_GROUP = 4
ATTN_HEADS = KV_HEADS * GQA_GROUP
ATTN_WIDTH = ATTN_HEADS * HEAD_DIM
WINDOW = 128
ROPE_THETA = 10000.0
GLA_HEADS = 4
GLA_DK = 128
GLA_DV = 256
GLA_KEY_WIDTH = GLA_HEADS * GLA_DK
GLA_WIDTH = GLA_HEADS * GLA_DV
GLA_DECAY_RANK = 16
GLA_GATE_NORMALIZER = 16.0
GLA_CHUNK = 64
GLA_BLOCK = 256
NORM_EPS = 1e-6

OFF_QA = 0
OFF_KA = OFF_QA + ATTN_WIDTH
OFF_VA = OFF_KA + KV_HEADS * HEAD_DIM
OFF_QG = OFF_VA + KV_HEADS * HEAD_DIM
OFF_KG = OFF_QG + GLA_KEY_WIDTH
OFF_VG = OFF_KG + GLA_KEY_WIDTH
OFF_GG = OFF_VG + GLA_WIDTH
OFF_LR = OFF_GG + GLA_WIDTH
MAIN_WIDTH = OFF_LR
LR_PAD = 128

VMEM_LIMIT_BYTES = 56 * 1024 * 1024
INPROJ_TM = 1024
INPROJ_TN = 1536
ATTN_TQ = 128
ROPE_ROWS = 512
OUTPROJ_TM = 512
FFN_TM = 1024
FFN_TF = 512
FFN_XROWS = 128
HALO = 16


def _cparams(*sem):
    return pltpu.CompilerParams(dimension_semantics=sem, vmem_limit_bytes=VMEM_LIMIT_BYTES)


def _inproj_kernel(x_ref, g_ref, w_ref, wlr_ref, o_ref, lr_ref, h_ref):
    @pl.when(pl.program_id(1) == 0)
    def _():
        x = x_ref[...]
        ms = jnp.mean(x * x, axis=-1, keepdims=True)
        h = (x * lax.rsqrt(ms + NORM_EPS) * g_ref[...]).astype(BF16)
        h_ref[...] = h
        lr_ref[...] = jnp.dot(h, wlr_ref[...], preferred_element_type=F32)

    o_ref[...] = jnp.dot(h_ref[...], w_ref[...], preferred_element_type=F32).astype(o_ref.dtype)


def _inproj(x2, g1, w_main, w_lr):
    m, d = x2.shape
    tm, tn = INPROJ_TM, INPROJ_TN
    return pl.pallas_call(
        _inproj_kernel,
        grid=(m // tm, MAIN_WIDTH // tn),
        in_specs=[
            pl.BlockSpec((tm, d), lambda i, j: (i, 0)),
            pl.BlockSpec((1, d), lambda i, j: (0, 0)),
            pl.BlockSpec((d, tn), lambda i, j: (0, j)),
            pl.BlockSpec((d, LR_PAD), lambda i, j: (0, 0)),
        ],
        out_specs=[
            pl.BlockSpec((tm, tn), lambda i, j: (i, j)),
            pl.BlockSpec((tm, LR_PAD), lambda i, j: (i, 0)),
        ],
        out_shape=[
            jax.ShapeDtypeStruct((m, MAIN_WIDTH), BF16),
            jax.ShapeDtypeStruct((m, LR_PAD), F32),
        ],
        scratch_shapes=[pltpu.VMEM((tm, d), BF16)],
        compiler_params=_cparams("parallel", "arbitrary"),
        name="inproj",
    )(x2, g1, w_main, w_lr)


def _norm_rope(t, gain, cos, sin):
    t = t * lax.rsqrt(jnp.mean(t * t, axis=-1, keepdims=True) + NORM_EPS) * gain
    return t * cos + pltpu.roll(t, HEAD_DIM // 2, 1) * sin


def _attn_kernel(sink_ref, q_ref, k_ref, v_ref, cos_ref, sin_ref, qg_ref, kg_ref, o_ref, kr_ref,
                 *, tq, seq):
    i = pl.program_id(1)

    @pl.when(i == 0)
    def _():
        def body(c, carry):
            r0 = pl.multiple_of(c * ROPE_ROWS, ROPE_ROWS)
            cs = cos_ref[pl.ds(r0, ROPE_ROWS), :]
            sn = sin_ref[pl.ds(r0, ROPE_ROWS), :]
            for h in range(KV_HEADS):
                cols = slice(h * HEAD_DIM, (h + 1) * HEAD_DIM)
                kk = k_ref[0, pl.ds(r0, ROPE_ROWS), cols].astype(F32)
                kr_ref[pl.ds(r0, ROPE_ROWS), cols] = _norm_rope(kk, kg_ref[...], cs, sn).astype(BF16)
            return carry

        lax.fori_loop(0, seq // ROPE_ROWS, body, 0)

    win = tq + 2 * WINDOW
    q0 = pl.multiple_of(i * tq, tq)
    start = pl.multiple_of(jnp.clip(q0 - WINDOW, 0, seq - win), WINDOW)
    cs = cos_ref[pl.ds(q0, tq), :]
    sn = sin_ref[pl.ds(q0, tq), :]
    row = lax.broadcasted_iota(jnp.int32, (tq, win), 0)
    col = lax.broadcasted_iota(jnp.int32, (tq, win), 1)
    mask = (jnp.abs((start + col) - (q0 + row)) <= WINDOW)[None]
    grp = lax.broadcasted_iota(jnp.int32, (GQA_GROUP, 1, 1), 0)
    scale = HEAD_DIM ** -0.5

    for kvh in range(KV_HEADS):
        cols = slice(kvh * HEAD_DIM, (kvh + 1) * HEAD_DIM)
        kw = kr_ref[pl.ds(start, win), cols]
        vw = v_ref[0, pl.ds(start, win), cols]
        qs = []
        sink = jnp.zeros((GQA_GROUP, 1, 1), F32)
        for g in range(GQA_GROUP):
            h = kvh * GQA_GROUP + g
            qq = q_ref[0, :, h * HEAD_DIM:(h + 1) * HEAD_DIM].astype(F32)
            qs.append((_norm_rope(qq, qg_ref[...], cs, sn) * scale).astype(BF16))
            sink = jnp.where(grp == g, sink_ref[h], sink)
        qst = jnp.concatenate(qs, axis=0)
        s = lax.dot_general(qst, kw, (((1,), (1,)), ((), ())), preferred_element_type=F32)
        s = jnp.where(mask, s.reshape(GQA_GROUP, tq, win), -jnp.inf)
        m = jnp.maximum(jnp.max(s, axis=-1, keepdims=True), sink)
        p = jnp.exp(s - m)
        denom = jnp.sum(p, axis=-1, keepdims=True) + jnp.exp(sink - m)
        o = jnp.dot(p.astype(BF16).reshape(GQA_GROUP * tq, win), vw, preferred_element_type=F32)
        o = o.reshape(GQA_GROUP, tq, HEAD_DIM) / denom
        for g in range(GQA_GROUP):
            h = kvh * GQA_GROUP + g
            o_ref[0, :, h * HEAD_DIM:(h + 1) * HEAD_DIM] = o[g].astype(o_ref.dtype)


def _attention(proj3, cos, sin, qg, kg, sink):
    b, seq, _ = proj3.shape
    tq = ATTN_TQ
    kvw = KV_HEADS * HEAD_DIM
    return pl.pallas_call(
        functools.partial(_attn_kernel, tq=tq, seq=seq),
        grid=(b, seq // tq),
        in_specs=[
            pl.BlockSpec(memory_space=pltpu.SMEM),
            pl.BlockSpec((1, tq, ATTN_WIDTH), lambda bi, i: (bi, i, OFF_QA // ATTN_WIDTH)),
            pl.BlockSpec((1, seq, kvw), lambda bi, i: (bi, 0, OFF_KA // kvw)),
            pl.BlockSpec((1, seq, kvw), lambda bi, i: (bi, 0, OFF_VA // kvw)),
            pl.BlockSpec((seq, HEAD_DIM), lambda bi, i: (0, 0)),
            pl.BlockSpec((seq, HEAD_DIM), lambda bi, i: (0, 0)),
            pl.BlockSpec((1, HEAD_DIM), lambda bi, i: (0, 0)),
            pl.BlockSpec((1, HEAD_DIM), lambda bi, i: (0, 0)),
        ],
        out_specs=pl.BlockSpec((1, tq, ATTN_WIDTH), lambda bi, i: (bi, i, 0)),
        out_shape=jax.ShapeDtypeStruct((b, seq, ATTN_WIDTH), BF16),
        scratch_shapes=[pltpu.VMEM((seq, kvw), BF16)],
        compiler_params=_cparams("parallel", "arbitrary"),
        name="swa",
    )(sink, proj3, proj3, proj3, cos, sin, qg, kg)


def _log_sigmoid(z):
    return -(jnp.maximum(-z, 0.0) + jnp.log1p(jnp.exp(-jnp.abs(z))))


def _split2(g):
    hi = g.astype(BF16)
    lo = (g - hi.astype(F32)).astype(BF16)
    return hi, lo


def _gla_kernel(q_ref, k_ref, v_ref, gate_ref, lr_ref, wa_ref, ba_ref, gn_ref,
                o_ref, qin_ref, kv_ref, dec_ref, acc_ref, st_ref, *, seq):
    c = GLA_CHUNK
    n = seq // c
    blk = GLA_BLOCK
    cpb = blk // c

    ri = lax.broadcasted_iota(jnp.int32, (blk, blk), 0)
    ci = lax.broadcasted_iota(jnp.int32, (blk, blk), 1)
    same = (ri // c) == (ci // c)
    lower_incl = same & (ci <= ri)
    upper_incl = same & (ci >= ri)
    upper_strict = same & (ci > ri)
    tri_f = jnp.where(lower_incl, 1.0, 0.0).astype(BF16)
    tri_b = jnp.where(upper_incl, 1.0, 0.0).astype(BF16)
    qscale = GLA_DK ** -0.5

    def intra(t, carry):
        r0 = pl.multiple_of(t * blk, blk)
        lr = lr_ref[0, pl.ds(r0, blk), :].astype(BF16)
        q3 = (q_ref[0, pl.ds(r0, blk), :].astype(F32) * qscale).reshape(cpb, c, GLA_DK)
        k3 = k_ref[0, pl.ds(r0, blk), :].astype(F32).reshape(cpb, c, GLA_DK)
        v = v_ref[0, pl.ds(r0, blk), :]

        z = jnp.dot(lr, wa_ref[...], preferred_element_type=F32) + ba_ref[...]
        g_hi, g_lo = _split2(_log_sigmoid(z) / GLA_GATE_NORMALIZER)

        def one_direction(d, tri, keep, ref_row, last_row, slot):
            cols = slice(d * GLA_DK, (d + 1) * GLA_DK)
            parts = jnp.concatenate([g_hi[:, cols], g_lo[:, cols]], axis=1)
            bsum = jnp.dot(tri, parts, preferred_element_type=F32)
            bcum = (bsum[:, :GLA_DK] + bsum[:, GLA_DK:]).reshape(cpb, c, GLA_DK)
            b_mid = bcum[:, ref_row:ref_row + 1, :]
            b_end = bcum[:, last_row:last_row + 1, :]
            qe = q3 * jnp.exp(bcum - b_mid)
            ke = k3 * jnp.exp(b_mid - bcum)
            qin_ref[slot, pl.ds(r0, blk), :] = (qe * jnp.exp(b_mid)).astype(BF16).reshape(blk, GLA_DK)
            kout = (ke * jnp.exp(b_end - b_mid)).astype(BF16)
            for j in range(cpb):
                kv_ref[slot, t * cpb + j] = lax.dot_general(
                    v[j * c:(j + 1) * c], kout[j], (((0,), (0,)), ((), ())), preferred_element_type=F32)
            dec = jnp.broadcast_to(jnp.exp(b_end), (cpb, 8, GLA_DK)).reshape(cpb * 8, GLA_DK)
            dec_ref[slot, pl.ds(pl.multiple_of(t * (cpb * 8), cpb * 8), cpb * 8), :] = dec
            a = lax.dot_general(qe.astype(BF16).reshape(blk, GLA_DK), ke.astype(BF16).reshape(blk, GLA_DK),
                                (((1,), (1,)), ((), ())), preferred_element_type=F32)
            a = jnp.where(keep, a, 0.0).astype(BF16)
            return jnp.dot(a, v, preferred_element_type=F32)

        o = one_direction(0, tri_f, lower_incl, c // 2, c - 1, 0)
        o = o + one_direction(1, tri_b, upper_strict, c - 1 - c // 2, 0, 1)
        acc_ref[pl.ds(r0, blk), :] = o
        return carry

    lax.fori_loop(0, seq // blk, intra, 0, unroll=4)

    def step(idx, slot):
        r0 = pl.multiple_of(idx * c, c)
        st = st_ref[slot]
        o = lax.dot_general(qin_ref[slot, pl.ds(r0, c), :], st.astype(BF16), (((1,), (1,)), ((), ())),
                            preferred_element_type=F32)
        dec = dec_ref[slot, pl.ds(pl.multiple_of(idx * 8, 8), 8), :]
        st_ref[slot] = ((st.reshape(GLA_DV // 8, 8, GLA_DK) * dec[None]).reshape(GLA_DV, GLA_DK)
                        + kv_ref[slot, idx])
        return r0, o

    def accumulate(r0, o):
        acc_ref[pl.ds(r0, c), :] += o

    def finalize(r0, o):
        o = o + acc_ref[pl.ds(r0, c), :]
        y = o * lax.rsqrt(jnp.mean(o * o, axis=-1, keepdims=True) + NORM_EPS) * gn_ref[...]
        gate = gate_ref[0, pl.ds(r0, c), :].astype(F32)
        o_ref[0, pl.ds(r0, c), :] = (y * (gate * jax.nn.sigmoid(gate))).astype(o_ref.dtype)

    st_ref[...] = jnp.zeros_like(st_ref)

    def first_half(t, carry):
        accumulate(*step(t, 0))
        accumulate(*step(n - 1 - t, 1))
        return carry

    def second_half(t, carry):
        finalize(*step(t, 0))
        finalize(*step(n - 1 - t, 1))
        return carry

    lax.fori_loop(0, n // 2, first_half, 0, unroll=4)
    lax.fori_loop(n // 2, n, second_half, 0, unroll=4)


def _gla(proj3, lr3, wa, ba, gn):
    b, seq, _ = proj3.shape
    assert (seq // GLA_CHUNK) % 8 == 0 and seq % GLA_BLOCK == 0
    return pl.pallas_call(
        functools.partial(_gla_kernel, seq=seq),
        grid=(b, GLA_HEADS),
        in_specs=[
            pl.BlockSpec((1, seq, GLA_DK), lambda bi, h: (bi, 0, OFF_QG // GLA_DK + h)),
            pl.BlockSpec((1, seq, GLA_DK), lambda bi, h: (bi, 0, OFF_KG // GLA_DK + h)),
            pl.BlockSpec((1, seq, GLA_DV), lambda bi, h: (bi, 0, OFF_VG // GLA_DV + h)),
            pl.BlockSpec((1, seq, GLA_DV), lambda bi, h: (bi, 0, OFF_GG // GLA_DV + h)),
            pl.BlockSpec((1, seq, LR_PAD), lambda bi, h: (bi, 0, 0)),
            pl.BlockSpec((LR_PAD, 2 * GLA_DK), lambda bi, h: (0, h)),
            pl.BlockSpec((1, 2 * GLA_DK), lambda bi, h: (0, h)),
            pl.BlockSpec((1, GLA_DV), lambda bi, h: (0, 0)),
        ],
        out_specs=pl.BlockSpec((1, seq, GLA_DV), lambda bi, h: (bi, 0, h)),
        out_shape=jax.ShapeDtypeStruct((b, seq, GLA_WIDTH), BF16),
        scratch_shapes=[
            pltpu.VMEM((2, seq, GLA_DK), BF16),
            pltpu.VMEM((2, seq // GLA_CHUNK, GLA_DV, GLA_DK), F32),
            pltpu.VMEM((2, seq // GLA_CHUNK * 8, GLA_DK), F32),
            pltpu.VMEM((seq, GLA_DV), F32),
            pltpu.VMEM((2, GLA_DV, GLA_DK), F32),
        ],
        compiler_params=_cparams("parallel", "arbitrary"),
        name="gla",
    )(proj3, proj3, proj3, proj3, lr3, wa, ba, gn)


def _outproj_kernel(oa_ref, og_ref, x_ref, w_ref, g_ref, x1_ref, h2_ref):
    wa = oa_ref.shape[1]
    acc = jnp.dot(oa_ref[...], w_ref[:wa, :], preferred_element_type=F32)
    acc = acc + jnp.dot(og_ref[...], w_ref[wa:, :], preferred_element_type=F32)
    x1 = x_ref[...] + acc
    x1_ref[...] = x1
    ms = jnp.mean(x1 * x1, axis=-1, keepdims=True)
    h2_ref[...] = (x1 * lax.rsqrt(ms + NORM_EPS) * g_ref[...]).astype(h2_ref.dtype)


def _outproj(oa, og, x2, w_out, g2):
    m, d = x2.shape
    tm = OUTPROJ_TM
    wa, wg = oa.shape[1], og.shape[1]
    return pl.pallas_call(
        _outproj_kernel,
        grid=(m // tm,),
        in_specs=[
            pl.BlockSpec((tm, wa), lambda i: (i, 0)),
            pl.BlockSpec((tm, wg), lambda i: (i, 0)),
            pl.BlockSpec((tm, d), lambda i: (i, 0)),
            pl.BlockSpec((wa + wg, d), lambda i: (0, 0)),
            pl.BlockSpec((1, d), lambda i: (0, 0)),
        ],
        out_specs=[
            pl.BlockSpec((tm, d), lambda i: (i, 0)),
            pl.BlockSpec((tm, d), lambda i: (i, 0)),
        ],
        out_shape=[
            jax.ShapeDtypeStruct((m, d), F32),
            jax.ShapeDtypeStruct((m, d), BF16),
        ],
        compiler_params=_cparams("parallel"),
        name="outproj",
    )(oa, og, x2, w_out, g2)


def _ffn_kernel(h_ref, hp_ref, hn_ref, x1_ref, wg_ref, wv_ref, cwg_ref, cwv_ref, cbg_ref, cbv_ref,
                wd_ref, o_ref, hext_ref, *, tm, tiles_per_seq):
    i = pl.program_id(0)
    f = pl.program_id(1)

    th = tm // 2
    rows = th + HALO

    @pl.when(f == 0)
    def _():
        pos = i % tiles_per_seq
        r = lax.broadcasted_iota(jnp.int32, (HALO, 1), 0)
        zero = jnp.zeros_like(hn_ref[...])
        prev0 = jnp.where((r == HALO - 1) & (pos != 0), hp_ref[...], zero)
        next1 = jnp.where((r == 0) & (pos != tiles_per_seq - 1), hn_ref[...], zero)
        hext_ref[0, :th, :] = h_ref[:th, :]
        hext_ref[0, th:, :] = jnp.where(r == 0, h_ref[th:th + HALO, :], prev0)
        hext_ref[1, :th, :] = h_ref[th:, :]
        hext_ref[1, th:, :] = jnp.where(r == HALO - 1, h_ref[th - HALO:th, :], next1)
        o_ref[...] = jnp.zeros_like(o_ref)

    def conv(u, cw_ref, cb_ref):
        prev = pltpu.roll(u, 1, 0)
        nxt = pltpu.roll(u, rows - 1, 0)
        y = prev * cw_ref[0:1, :] + u * cw_ref[1:2, :] + nxt * cw_ref[2:3, :] + cb_ref[...]
        return y[:th]

    acts = []
    for half in range(2):
        hext = hext_ref[half]
        ug = conv(jnp.dot(hext, wg_ref[...], preferred_element_type=F32), cwg_ref, cbg_ref)
        uv = conv(jnp.dot(hext, wv_ref[...], preferred_element_type=F32), cwv_ref, cbv_ref)
        acts.append((ug * jax.nn.sigmoid(ug) * uv).astype(BF16))
    for half in range(2):
        o_ref[half * th:(half + 1) * th, :] += jnp.dot(acts[half], wd_ref[...], preferred_element_type=F32)

    @pl.when(f < tm // FFN_XROWS)
    def _():
        r0 = pl.multiple_of(f * FFN_XROWS, FFN_XROWS)
        o_ref[pl.ds(r0, FFN_XROWS), :] += x1_ref[...]


def _ffn(h2, x1, w_up, conv_w, conv_b, w_down, seq):
    m, d = h2.shape
    dff = w_down.shape[0]
    tm, tf = FFN_TM, FFN_TF
    nf = dff // tf
    hb = tm // HALO
    last_halo = m // HALO - 1
    xch = tm // FFN_XROWS
    assert nf >= xch
    return pl.pallas_call(
        functools.partial(_ffn_kernel, tm=tm, tiles_per_seq=seq // tm),
        grid=(m // tm, nf),
        in_specs=[
            pl.BlockSpec((tm, d), lambda i, f: (i, 0)),
            pl.BlockSpec((HALO, d), lambda i, f: (jnp.maximum(i * hb - 1, 0), 0)),
            pl.BlockSpec((HALO, d), lambda i, f: (jnp.minimum((i + 1) * hb, last_halo), 0)),
            pl.BlockSpec((FFN_XROWS, d), lambda i, f: (i * xch + jnp.minimum(f, xch - 1), 0)),
            pl.BlockSpec((d, tf), lambda i, f: (0, f)),
            pl.BlockSpec((d, tf), lambda i, f: (0, nf + f)),
            pl.BlockSpec((3, tf), lambda i, f: (0, f)),
            pl.BlockSpec((3, tf), lambda i, f: (0, nf + f)),
            pl.BlockSpec((1, tf), lambda i, f: (0, f)),
            pl.BlockSpec((1, tf), lambda i, f: (0, nf + f)),
            pl.BlockSpec((tf, d), lambda i, f: (f, 0)),
        ],
        out_specs=pl.BlockSpec((tm, d), lambda i, f: (i, 0)),
        out_shape=jax.ShapeDtypeStruct((m, d), F32),
        scratch_shapes=[
            pltpu.VMEM((2, tm // 2 + HALO, d), BF16),
        ],
        compiler_params=_cparams("parallel", "arbitrary"),
        name="convffn",
    )(h2, h2, h2, x1, w_up, w_up, conv_w, conv_w, conv_b, conv_b, w_down)


def _rope_tables(seq):
    half = HEAD_DIM // 2
    inv = 1.0 / (ROPE_THETA ** (jnp.arange(half, dtype=F32) / half))
    ang = jnp.arange(seq, dtype=F32)[:, None] * inv[None, :]
    cos = jnp.cos(ang)
    sin = jnp.sin(ang)
    return jnp.concatenate([cos, cos], axis=-1), jnp.concatenate([-sin, sin], axis=-1)


def _decay_params(wa2_f, ba_f, wa2_b, ba_b):
    r = GLA_DECAY_RANK
    wf = jnp.pad(wa2_f, ((0, LR_PAD - r), (0, 0))).reshape(LR_PAD, GLA_HEADS, 1, GLA_DK)
    wb = jnp.pad(wa2_b, ((r, LR_PAD - 2 * r), (0, 0))).reshape(LR_PAD, GLA_HEADS, 1, GLA_DK)
    wa = jnp.concatenate([wf, wb], axis=2).reshape(LR_PAD, 2 * GLA_KEY_WIDTH).astype(BF16)
    ba = jnp.concatenate([ba_f.reshape(GLA_HEADS, 1, GLA_DK), ba_b.reshape(GLA_HEADS, 1, GLA_DK)],
                         axis=1).reshape(1, 2 * GLA_KEY_WIDTH)
    return wa, ba


def _layer(x, norm1_g, w_in, q_norm_g, k_norm_g, sink, wa2_f, ba_f, wa2_b, ba_b, gla_norm_g,
           w_out, norm2_g, w_up, conv_w, conv_b, w_down):
    b, seq, d = x.shape
    m = b * seq
    x2 = x.reshape(m, d)
    w_lr = jnp.pad(w_in[:, MAIN_WIDTH:], ((0, 0), (0, LR_PAD - 2 * GLA_DECAY_RANK))).astype(BF16)
    proj, lr = _inproj(x2, norm1_g.reshape(1, d), w_in.astype(BF16), w_lr)
    proj3 = proj.reshape(b, seq, MAIN_WIDTH)
    cos, sin = _rope_tables(seq)
    o_attn = _attention(proj3, cos, sin, q_norm_g.reshape(1, HEAD_DIM), k_norm_g.reshape(1, HEAD_DIM), sink)
    wa, ba = _decay_params(wa2_f, ba_f, wa2_b, ba_b)
    o_gla = _gla(proj3, lr.reshape(b, seq, LR_PAD), wa, ba, gla_norm_g.reshape(1, GLA_DV))
    x1, h2 = _outproj(o_attn.reshape(m, ATTN_WIDTH), o_gla.reshape(m, GLA_WIDTH), x2,
                      w_out.astype(BF16), norm2_g.reshape(1, d))
    out = _ffn(h2, x1, w_up.astype(BF16), conv_w, conv_b.reshape(1, -1), w_down.astype(BF16), seq)
    return out.reshape(b, seq, d)


def kernel(x, norm1_g, w_in, attn_q_norm_g, attn_k_norm_g, attn_sink, gla_wa2_fwd, gla_ba_fwd, gla_wa2_bwd, gla_ba_bwd, gla_out_norm_g, w_out, norm2_g, w_up, conv_w, conv_b, w_down):
    for l in range(w_in.shape[0]):
        x = _layer(x, norm1_g[l], w_in[l], attn_q_norm_g[l], attn_k_norm_g[l], attn_sink[l],
                   gla_wa2_fwd[l], gla_ba_fwd[l], gla_wa2_bwd[l], gla_ba_bwd[l], gla_out_norm_g[l],
                   w_out[l], norm2_g[l], w_up[l], conv_w[l], conv_b[l], w_down[l])
    return x
```

```python
import functools

import jax
import jax.numpy as jnp
from jax import lax
from jax.experimental import pallas as pl
from jax.experimental.pallas import tpu as pltpu

F32 = jnp.float32
BF16 = jnp.bfloat16

HEAD_DIM = 128
KV_HEADS = 2
GQA_GROUP = 4
ATTN_HEADS = KV_HEADS * GQA_GROUP
ATTN_WIDTH = ATTN_HEADS * HEAD_DIM
WINDOW = 128
ROPE_THETA = 10000.0
GLA_HEADS = 4
GLA_DK = 128
GLA_DV = 256
GLA_KEY_WIDTH = GLA_HEADS * GLA_DK
GLA_WIDTH = GLA_HEADS * GLA_DV
GLA_DECAY_RANK = 16
GLA_GATE_NORMALIZER = 16.0
GLA_CHUNK = 64
GLA_BLOCK = 256
NORM_EPS = 1e-6

OFF_QA = 0
OFF_KA = OFF_QA + ATTN_WIDTH
OFF_VA = OFF_KA + KV_HEADS * HEAD_DIM
OFF_QG = OFF_VA + KV_HEADS * HEAD_DIM
OFF_KG = OFF_QG + GLA_KEY_WIDTH
OFF_VG = OFF_KG + GLA_KEY_WIDTH
OFF_GG = OFF_VG + GLA_WIDTH
OFF_LR = OFF_GG + GLA_WIDTH
MAIN_WIDTH = OFF_LR
LR_PAD = 128

VMEM_LIMIT_BYTES = 56 * 1024 * 1024
INPROJ_TM = 1024
INPROJ_TN = 1536
ATTN_TQ = 128
ROPE_ROWS = 512
OUTPROJ_TM = 512
FFN_TM = 1024
FFN_TF = 512
FFN_XROWS = 128
HALO = 16


def _cparams(*sem):
    return pltpu.CompilerParams(dimension_semantics=sem, vmem_limit_bytes=VMEM_LIMIT_BYTES)


def _inproj_kernel(x_ref, g_ref, w_ref, wlr_ref, o_ref, lr_ref, h_ref):
    @pl.when(pl.program_id(1) == 0)
    def _():
        x = x_ref[...]
        ms = jnp.mean(x * x, axis=-1, keepdims=True)
        h = (x * lax.rsqrt(ms + NORM_EPS) * g_ref[...]).astype(BF16)
        h_ref[...] = h
        o_ref[...] = jnp.dot(h, w_ref[...], preferred_element_type=F32).astype(o_ref.dtype)
        lr_ref[...] = jnp.dot(h, wlr_ref[...], preferred_element_type=F32)

    @pl.when(pl.program_id(1) > 0)
    def _():
        o_ref[...] = jnp.dot(h_ref[...], w_ref[...], preferred_element_type=F32).astype(o_ref.dtype)


def _inproj(x2, g1, w_main, w_lr):
    m, d = x2.shape
    tm, tn = INPROJ_TM, INPROJ_TN
    return pl.pallas_call(
        _inproj_kernel,
        grid=(m // tm, MAIN_WIDTH // tn),
        in_specs=[
            pl.BlockSpec((tm, d), lambda i, j: (i, 0)),
            pl.BlockSpec((1, d), lambda i, j: (0, 0)),
            pl.BlockSpec((d, tn), lambda i, j: (0, j)),
            pl.BlockSpec((d, LR_PAD), lambda i, j: (0, 0)),
        ],
        out_specs=[
            pl.BlockSpec((tm, tn), lambda i, j: (i, j)),
            pl.BlockSpec((tm, LR_PAD), lambda i, j: (i, 0)),
        ],
        out_shape=[
            jax.ShapeDtypeStruct((m, MAIN_WIDTH), BF16),
            jax.ShapeDtypeStruct((m, LR_PAD), F32),
        ],
        scratch_shapes=[pltpu.VMEM((tm, d), BF16)],
        compiler_params=_cparams("parallel", "arbitrary"),
        name="inproj",
    )(x2, g1, w_main, w_lr)


def _norm_rope(t, gain, cos, sin):
    t = t * lax.rsqrt(jnp.mean(t * t, axis=-1, keepdims=True) + NORM_EPS) * gain
    return t * cos + pltpu.roll(t, HEAD_DIM // 2, 1) * sin


def _attn_kernel(sink_ref, q_ref, k_ref, v_ref, cos_ref, sin_ref, qg_ref, kg_ref, o_ref, kr_ref,
                 *, tq, seq):
    i = pl.program_id(1)

    @pl.when(i == 0)
    def _():
        def body(c, carry):
            r0 = pl.multiple_of(c * ROPE_ROWS, ROPE_ROWS)
            cs = cos_ref[pl.ds(r0, ROPE_ROWS), :]
            sn = sin_ref[pl.ds(r0, ROPE_ROWS), :]
            for h in range(KV_HEADS):
                cols = slice(h * HEAD_DIM, (h + 1) * HEAD_DIM)
                kk = k_ref[0, pl.ds(r0, ROPE_ROWS), cols].astype(F32)
                kr_ref[pl.ds(r0, ROPE_ROWS), cols] = _norm_rope(kk, kg_ref[...], cs, sn).astype(BF16)
            return carry

        lax.fori_loop(0, seq // ROPE_ROWS, body, 0)

    win = tq + 2 * WINDOW
    q0 = pl.multiple_of(i * tq, tq)
    start = pl.multiple_of(jnp.clip(q0 - WINDOW, 0, seq - win), WINDOW)
    cs = cos_ref[pl.ds(q0, tq), :]
    sn = sin_ref[pl.ds(q0, tq), :]
    row = lax.broadcasted_iota(jnp.int32, (tq, win), 0)
    col = lax.broadcasted_iota(jnp.int32, (tq, win), 1)
    mask = (jnp.abs((start + col) - (q0 + row)) <= WINDOW)[None]
    grp = lax.broadcasted_iota(jnp.int32, (GQA_GROUP, 1, 1), 0)
    scale = HEAD_DIM ** -0.5

    for kvh in range(KV_HEADS):
        cols = slice(kvh * HEAD_DIM, (kvh + 1) * HEAD_DIM)
        kw = kr_ref[pl.ds(start, win), cols]
        vw = v_ref[0, pl.ds(start, win), cols]
        qs = []
        sink = jnp.zeros((GQA_GROUP, 1, 1), F32)
        for g in range(GQA_GROUP):
            h = kvh * GQA_GROUP + g
            qq = q_ref[0, :, h * HEAD_DIM:(h + 1) * HEAD_DIM].astype(F32)
            qs.append((_norm_rope(qq, qg_ref[...], cs, sn) * scale).astype(BF16))
            sink = jnp.where(grp == g, sink_ref[h], sink)
        qst = jnp.concatenate(qs, axis=0)
        s = lax.dot_general(qst, kw, (((1,), (1,)), ((), ())), preferred_element_type=F32)
        s = jnp.where(mask, s.reshape(GQA_GROUP, tq, win), -jnp.inf)
        m = jnp.maximum(jnp.max(s, axis=-1, keepdims=True), sink)
        p = jnp.exp(s - m)
        denom = jnp.sum(p, axis=-1, keepdims=True) + jnp.exp(sink - m)
        o = jnp.dot(p.astype(BF16).reshape(GQA_GROUP * tq, win), vw, preferred_element_type=F32)
        o = o.reshape(GQA_GROUP, tq, HEAD_DIM) / denom
        for g in range(GQA_GROUP):
            h = kvh * GQA_GROUP + g
            o_ref[0, :, h * HEAD_DIM:(h + 1) * HEAD_DIM] = o[g].astype(o_ref.dtype)


def _attention(proj3, cos, sin, qg, kg, sink):
    b, seq, _ = proj3.shape
    tq = ATTN_TQ
    kvw = KV_HEADS * HEAD_DIM
    return pl.pallas_call(
        functools.partial(_attn_kernel, tq=tq, seq=seq),
        grid=(b, seq // tq),
        in_specs=[
            pl.BlockSpec(memory_space=pltpu.SMEM),
            pl.BlockSpec((1, tq, ATTN_WIDTH), lambda bi, i: (bi, i, OFF_QA // ATTN_WIDTH)),
            pl.BlockSpec((1, seq, kvw), lambda bi, i: (bi, 0, OFF_KA // kvw)),
            pl.BlockSpec((1, seq, kvw), lambda bi, i: (bi, 0, OFF_VA // kvw)),
            pl.BlockSpec((seq, HEAD_DIM), lambda bi, i: (0, 0)),
            pl.BlockSpec((seq, HEAD_DIM), lambda bi, i: (0, 0)),
            pl.BlockSpec((1, HEAD_DIM), lambda bi, i: (0, 0)),
            pl.BlockSpec((1, HEAD_DIM), lambda bi, i: (0, 0)),
        ],
        out_specs=pl.BlockSpec((1, tq, ATTN_WIDTH), lambda bi, i: (bi, i, 0)),
        out_shape=jax.ShapeDtypeStruct((b, seq, ATTN_WIDTH), BF16),
        scratch_shapes=[pltpu.VMEM((seq, kvw), BF16)],
        compiler_params=_cparams("parallel", "arbitrary"),
        name="swa",
    )(sink, proj3, proj3, proj3, cos, sin, qg, kg)


def _log_sigmoid(z):
    return jnp.minimum(z, 0.0) - jnp.log(1.0 + jnp.exp(-jnp.abs(z)))


def _split2(g):
    hi = g.astype(BF16)
    lo = (g - hi.astype(F32)).astype(BF16)
    return hi, lo


def _gla_kernel(q_ref, k_ref, v_ref, gate_ref, lr_ref, wa_ref, ba_ref, gn_ref,
                o_ref, qin_ref, kv_ref, dec_ref, acc_ref, st_ref, *, seq):
    c = GLA_CHUNK
    n = seq // c
    blk = GLA_BLOCK
    cpb = blk // c

    ri = lax.broadcasted_iota(jnp.int32, (blk, blk), 0)
    ci = lax.broadcasted_iota(jnp.int32, (blk, blk), 1)
    same = (ri // c) == (ci // c)
    lower_incl = same & (ci <= ri)
    upper_strict = same & (ci > ri)
    tri_f = jnp.where(lower_incl, 1.0, 0.0).astype(BF16)
    qscale = GLA_DK ** -0.5

    def intra(t, carry):
        r0 = pl.multiple_of(t * blk, blk)
        lr = lr_ref[0, pl.ds(r0, blk), :].astype(BF16)
        q3 = (q_ref[0, pl.ds(r0, blk), :].astype(F32) * qscale).reshape(cpb, c, GLA_DK)
        k3 = k_ref[0, pl.ds(r0, blk), :].astype(F32).reshape(cpb, c, GLA_DK)
        v = v_ref[0, pl.ds(r0, blk), :]

        z = jnp.dot(lr, wa_ref[...], preferred_element_type=F32) + ba_ref[...]
        g = _log_sigmoid(z) / GLA_GATE_NORMALIZER
        g_hi, g_lo = _split2(g)
        pre = jnp.dot(tri_f, jnp.concatenate([g_hi, g_lo], axis=1), preferred_element_type=F32)
        prefix = pre[:, :2 * GLA_DK] + pre[:, 2 * GLA_DK:]
        bcum_f = prefix[:, :GLA_DK].reshape(cpb, c, GLA_DK)
        pref_b = prefix[:, GLA_DK:].reshape(cpb, c, GLA_DK)
        bcum_b = pref_b[:, c - 1:c, :] - pref_b + g[:, GLA_DK:].reshape(cpb, c, GLA_DK)

        def one_direction(bcum, keep, ref_row, last_row, slot):
            b_mid = bcum[:, ref_row:ref_row + 1, :]
            b_end = bcum[:, last_row:last_row + 1, :]
            qe = q3 * jnp.exp(bcum - b_mid)
            ke = k3 * jnp.exp(b_mid - bcum)
            qin_ref[slot, pl.ds(r0, blk), :] = (qe * jnp.exp(b_mid)).astype(BF16).reshape(blk, GLA_DK)
            kout = (ke * jnp.exp(b_end - b_mid)).astype(BF16)
            for j in range(cpb):
                kv_ref[slot, t * cpb + j] = lax.dot_general(
                    v[j * c:(j + 1) * c], kout[j], (((0,), (0,)), ((), ())), preferred_element_type=F32)
            dec = jnp.broadcast_to(jnp.exp(b_end), (cpb, 8, GLA_DK)).reshape(cpb * 8, GLA_DK)
            dec_ref[slot, pl.ds(pl.multiple_of(t * (cpb * 8), cpb * 8), cpb * 8), :] = dec
            a = lax.dot_general(qe.astype(BF16).reshape(blk, GLA_DK), ke.astype(BF16).reshape(blk, GLA_DK),
                                (((1,), (1,)), ((), ())), preferred_element_type=F32)
            return jnp.where(keep, a, 0.0).astype(BF16)

        a_f = one_direction(bcum_f, lower_incl, c // 2, c - 1, 0)
        a_b = one_direction(bcum_b, upper_strict, c - 1 - c // 2, 0, 1)
        o = jnp.dot(jnp.concatenate([a_f, a_b], axis=1), jnp.concatenate([v, v], axis=0),
                    preferred_element_type=F32)
        acc_ref[pl.ds(r0, blk), :] = o
        return carry

    lax.fori_loop(0, seq // blk, intra, 0, unroll=4)

    def step(idx, slot):
        r0 = pl.multiple_of(idx * c, c)
        st = st_ref[slot]
        o = lax.dot_general(qin_ref[slot, pl.ds(r0, c), :], st.astype(BF16), (((1,), (1,)), ((), ())),
                            preferred_element_type=F32)
        dec = dec_ref[slot, pl.ds(pl.multiple_of(idx * 8, 8), 8), :]
        st_ref[slot] = ((st.reshape(GLA_DV // 8, 8, GLA_DK) * dec[None]).reshape(GLA_DV, GLA_DK)
                        + kv_ref[slot, idx])
        return r0, o

    def accumulate(r0, o):
        acc_ref[pl.ds(r0, c), :] += o

    def finalize(r0, o):
        o = o + acc_ref[pl.ds(r0, c), :]
        y = o * lax.rsqrt(jnp.mean(o * o, axis=-1, keepdims=True) + NORM_EPS) * gn_ref[...]
        gate = gate_ref[0, pl.ds(r0, c), :].astype(F32)
        o_ref[0, pl.ds(r0, c), :] = (y * (gate * jax.nn.sigmoid(gate))).astype(o_ref.dtype)

    st_ref[...] = jnp.zeros_like(st_ref)

    def first_half(t, carry):
        accumulate(*step(t, 0))
        accumulate(*step(n - 1 - t, 1))
        return carry

    def second_half(t, carry):
        finalize(*step(t, 0))
        finalize(*step(n - 1 - t, 1))
        return carry

    lax.fori_loop(0, n // 2, first_half, 0, unroll=4)
    lax.fori_loop(n // 2, n, second_half, 0, unroll=4)


def _gla(proj3, lr3, wa, ba, gn):
    b, seq, _ = proj3.shape
    assert (seq // GLA_CHUNK) % 8 == 0 and seq % GLA_BLOCK == 0
    return pl.pallas_call(
        functools.partial(_gla_kernel, seq=seq),
        grid=(b, GLA_HEADS),
        in_specs=[
            pl.BlockSpec((1, seq, GLA_DK), lambda bi, h: (bi, 0, OFF_QG // GLA_DK + h)),
            pl.BlockSpec((1, seq, GLA_DK), lambda bi, h: (bi, 0, OFF_KG // GLA_DK + h)),
            pl.BlockSpec((1, seq, GLA_DV), lambda bi, h: (bi, 0, OFF_VG // GLA_DV + h)),
            pl.BlockSpec((1, seq, GLA_DV), lambda bi, h: (bi, 0, OFF_GG // GLA_DV + h)),
            pl.BlockSpec((1, seq, LR_PAD), lambda bi, h: (bi, 0, 0)),
            pl.BlockSpec((LR_PAD, 2 * GLA_DK), lambda bi, h: (0, h)),
            pl.BlockSpec((1, 2 * GLA_DK), lambda bi, h: (0, h)),
            pl.BlockSpec((1, GLA_DV), lambda bi, h: (0, 0)),
        ],
        out_specs=pl.BlockSpec((1, seq, GLA_DV), lambda bi, h: (bi, 0, h)),
        out_shape=jax.ShapeDtypeStruct((b, seq, GLA_WIDTH), BF16),
        scratch_shapes=[
            pltpu.VMEM((2, seq, GLA_DK), BF16),
            pltpu.VMEM((2, seq // GLA_CHUNK, GLA_DV, GLA_DK), F32),
            pltpu.VMEM((2, seq // GLA_CHUNK * 8, GLA_DK), F32),
            pltpu.VMEM((seq, GLA_DV), F32),
            pltpu.VMEM((2, GLA_DV, GLA_DK), F32),
        ],
        compiler_params=_cparams("parallel", "arbitrary"),
        name="gla",
    )(proj3, proj3, proj3, proj3, lr3, wa, ba, gn)


def _outproj_kernel(oa_ref, og_ref, x_ref, w_ref, g_ref, x1_ref, h2_ref):
    wa = oa_ref.shape[1]
    acc = jnp.dot(oa_ref[...], w_ref[:wa, :], preferred_element_type=F32)
    acc = acc + jnp.dot(og_ref[...], w_ref[wa:, :], preferred_element_type=F32)
    x1 = x_ref[...] + acc
    x1_ref[...] = x1
    ms = jnp.mean(x1 * x1, axis=-1, keepdims=True)
    h2_ref[...] = (x1 * lax.rsqrt(ms + NORM_EPS) * g_ref[...]).astype(h2_ref.dtype)


def _outproj(oa, og, x2, w_out, g2):
    m, d = x2.shape
    tm = OUTPROJ_TM
    wa, wg = oa.shape[1], og.shape[1]
    return pl.pallas_call(
        _outproj_kernel,
        grid=(m // tm,),
        in_specs=[
            pl.BlockSpec((tm, wa), lambda i: (i, 0)),
            pl.BlockSpec((tm, wg), lambda i: (i, 0)),
            pl.BlockSpec((tm, d), lambda i: (i, 0)),
            pl.BlockSpec((wa + wg, d), lambda i: (0, 0)),
            pl.BlockSpec((1, d), lambda i: (0, 0)),
        ],
        out_specs=[
            pl.BlockSpec((tm, d), lambda i: (i, 0)),
            pl.BlockSpec((tm, d), lambda i: (i, 0)),
        ],
        out_shape=[
            jax.ShapeDtypeStruct((m, d), F32),
            jax.ShapeDtypeStruct((m, d), BF16),
        ],
        compiler_params=_cparams("parallel"),
        name="outproj",
    )(oa, og, x2, w_out, g2)


def _ffn_kernel(h_ref, hp_ref, hn_ref, x1_ref, wg_ref, wv_ref, cwg_ref, cwv_ref, cbg_ref, cbv_ref,
                wd_ref, o_ref, hext_ref, *, tm, tiles_per_seq):
    i = pl.program_id(0)
    f = pl.program_id(1)

    th = tm // 2
    rows = th + HALO

    @pl.when(f == 0)
    def _():
        pos = i % tiles_per_seq
        r = lax.broadcasted_iota(jnp.int32, (HALO, 1), 0)
        zero = jnp.zeros_like(hn_ref[...])
        prev0 = jnp.where((r == HALO - 1) & (pos != 0), hp_ref[...], zero)
        next1 = jnp.where((r == 0) & (pos != tiles_per_seq - 1), hn_ref[...], zero)
        hext_ref[0, :th, :] = h_ref[:th, :]
        hext_ref[0, th:, :] = jnp.where(r == 0, h_ref[th:th + HALO, :], prev0)
        hext_ref[1, :th, :] = h_ref[th:, :]
        hext_ref[1, th:, :] = jnp.where(r == HALO - 1, h_ref[th - HALO:th, :], next1)
        o_ref[...] = jnp.zeros_like(o_ref)

    def conv(u, cw_ref, cb_ref):
        prev = pltpu.roll(u, 1, 0)
        nxt = pltpu.roll(u, rows - 1, 0)
        y = prev * cw_ref[0:1, :] + u * cw_ref[1:2, :] + nxt * cw_ref[2:3, :] + cb_ref[...]
        return y[:th]

    acts = []
    for half in range(2):
        hext = hext_ref[half]
        ug = conv(jnp.dot(hext, wg_ref[...], preferred_element_type=F32), cwg_ref, cbg_ref)
        uv = conv(jnp.dot(hext, wv_ref[...], preferred_element_type=F32), cwv_ref, cbv_ref)
        acts.append((ug * jax.nn.sigmoid(ug) * uv).astype(BF16))
    for half in range(2):
        o_ref[half * th:(half + 1) * th, :] += jnp.dot(acts[half], wd_ref[...], preferred_element_type=F32)

    @pl.when(f < tm // FFN_XROWS)
    def _():
        r0 = pl.multiple_of(f * FFN_XROWS, FFN_XROWS)
        o_ref[pl.ds(r0, FFN_XROWS), :] += x1_ref[...]


def _ffn(h2, x1, w_up, conv_w, conv_b, w_down, seq):
    m, d = h2.shape
    dff = w_down.shape[0]
    tm, tf = FFN_TM, FFN_TF
    nf = dff // tf
    hb = tm // HALO
    last_halo = m // HALO - 1
    xch = tm // FFN_XROWS
    assert nf >= xch
    return pl.pallas_call(
        functools.partial(_ffn_kernel, tm=tm, tiles_per_seq=seq // tm),
        grid=(m // tm, nf),
        in_specs=[
            pl.BlockSpec((tm, d), lambda i, f: (i, 0)),
            pl.BlockSpec((HALO, d), lambda i, f: (jnp.maximum(i * hb - 1, 0), 0)),
            pl.BlockSpec((HALO, d), lambda i, f: (jnp.minimum((i + 1) * hb, last_halo), 0)),
            pl.BlockSpec((FFN_XROWS, d), lambda i, f: (i * xch + jnp.minimum(f, xch - 1), 0)),
            pl.BlockSpec((d, tf), lambda i, f: (0, f)),
            pl.BlockSpec((d, tf), lambda i, f: (0, nf + f)),
            pl.BlockSpec((3, tf), lambda i, f: (0, f)),
            pl.BlockSpec((3, tf), lambda i, f: (0, nf + f)),
            pl.BlockSpec((1, tf), lambda i, f: (0, f)),
            pl.BlockSpec((1, tf), lambda i, f: (0, nf + f)),
            pl.BlockSpec((tf, d), lambda i, f: (f, 0)),
        ],
        out_specs=pl.BlockSpec((tm, d), lambda i, f: (i, 0)),
        out_shape=jax.ShapeDtypeStruct((m, d), F32),
        scratch_shapes=[
            pltpu.VMEM((2, tm // 2 + HALO, d), BF16),
        ],
        compiler_params=_cparams("parallel", "arbitrary"),
        name="convffn",
    )(h2, h2, h2, x1, w_up, w_up, conv_w, conv_w, conv_b, conv_b, w_down)


def _rope_tables(seq):
    half = HEAD_DIM // 2
    inv = 1.0 / (ROPE_THETA ** (jnp.arange(half, dtype=F32) / half))
    ang = jnp.arange(seq, dtype=F32)[:, None] * inv[None, :]
    cos = jnp.cos(ang)
    sin = jnp.sin(ang)
    return jnp.concatenate([cos, cos], axis=-1), jnp.concatenate([-sin, sin], axis=-1)


def _decay_params(wa2_f, ba_f, wa2_b, ba_b):
    r = GLA_DECAY_RANK
    wf = jnp.pad(wa2_f, ((0, LR_PAD - r), (0, 0))).reshape(LR_PAD, GLA_HEADS, 1, GLA_DK)
    wb = jnp.pad(wa2_b, ((r, LR_PAD - 2 * r), (0, 0))).reshape(LR_PAD, GLA_HEADS, 1, GLA_DK)
    wa = jnp.concatenate([wf, wb], axis=2).reshape(LR_PAD, 2 * GLA_KEY_WIDTH).astype(BF16)
    ba = jnp.concatenate([ba_f.reshape(GLA_HEADS, 1, GLA_DK), ba_b.reshape(GLA_HEADS, 1, GLA_DK)],
                         axis=1).reshape(1, 2 * GLA_KEY_WIDTH)
    return wa, ba


def _layer(x, norm1_g, w_in, q_norm_g, k_norm_g, sink, wa2_f, ba_f, wa2_b, ba_b, gla_norm_g,
           w_out, norm2_g, w_up, conv_w, conv_b, w_down):
    b, seq, d = x.shape
    m = b * seq
    x2 = x.reshape(m, d)
    w_lr = jnp.pad(w_in[:, MAIN_WIDTH:], ((0, 0), (0, LR_PAD - 2 * GLA_DECAY_RANK))).astype(BF16)
    proj, lr = _inproj(x2, norm1_g.reshape(1, d), w_in.astype(BF16), w_lr)
    proj3 = proj.reshape(b, seq, MAIN_WIDTH)
    cos, sin = _rope_tables(seq)
    o_attn = _attention(proj3, cos, sin, q_norm_g.reshape(1, HEAD_DIM), k_norm_g.reshape(1, HEAD_DIM), sink)
    wa, ba = _decay_params(wa2_f, ba_f, wa2_b, ba_b)
    o_gla = _gla(proj3, lr.reshape(b, seq, LR_PAD), wa, ba, gla_norm_g.reshape(1, GLA_DV))
    x1, h2 = _outproj(o_attn.reshape(m, ATTN_WIDTH), o_gla.reshape(m, GLA_WIDTH), x2,
                      w_out.astype(BF16), norm2_g.reshape(1, d))
    out = _ffn(h2, x1, w_up.astype(BF16), conv_w, conv_b.reshape(1, -1), w_down.astype(BF16), seq)
    return out.reshape(b, seq, d)


def kernel(x, norm1_g, w_in, attn_q_norm_g, attn_k_norm_g, attn_sink, gla_wa2_fwd, gla_ba_fwd, gla_wa2_bwd, gla_ba_bwd, gla_out_norm_g, w_out, norm2_g, w_up, conv_w, conv_b, w_down):
    for l in range(w_in.shape[0]):
        x = _layer(x, norm1_g[l], w_in[l], attn_q_norm_g[l], attn_k_norm_g[l], attn_sink[l],
                   gla_wa2_fwd[l], gla_ba_fwd[l], gla_wa2_bwd[l], gla_ba_bwd[l], gla_out_norm_g[l],
                   w_out[l], norm2_g[l], w_up[l], conv_w[l], conv_b[l], w_down[l])
    return x
```

```python
import functools

import jax
import jax.numpy as jnp
import numpy as np
from jax import lax
from jax.experimental import pallas as pl
from jax.experimental.pallas import tpu as pltpu

F32 = jnp.float32
BF16 = jnp.bfloat16

HEAD_DIM = 128
KV_HEADS = 2
GQA_GROUP = 4
ATTN_HEADS = KV_HEADS * GQA_GROUP
ATTN_WIDTH = ATTN_HEADS * HEAD_DIM
WINDOW = 128
ROPE_THETA = 10000.0
GLA_HEADS = 4
GLA_DK = 128
GLA_DV = 256
GLA_KEY_WIDTH = GLA_HEADS * GLA_DK
GLA_WIDTH = GLA_HEADS * GLA_DV
GLA_DECAY_RANK = 16
GLA_GATE_NORMALIZER = 16.0
GLA_CHUNK = 64
GLA_BLOCK = 256
NORM_EPS = 1e-6

OFF_QA = 0
OFF_KA = OFF_QA + ATTN_WIDTH
OFF_VA = OFF_KA + KV_HEADS * HEAD_DIM
OFF_QG = OFF_VA + KV_HEADS * HEAD_DIM
OFF_KG = OFF_QG + GLA_KEY_WIDTH
OFF_VG = OFF_KG + GLA_KEY_WIDTH
OFF_GG = OFF_VG + GLA_WIDTH
OFF_LR = OFF_GG + GLA_WIDTH
MAIN_WIDTH = OFF_LR
LR_PAD = 128

VMEM_LIMIT_BYTES = 56 * 1024 * 1024
INPROJ_TM = 1024
INPROJ_TN = 1536
ATTN_TQ = 128
ROPE_ROWS = 512
OUTPROJ_TM = 512
FFN_TM = 1024
FFN_TF = 512
FFN_XROWS = 128
HALO = 16


def _cparams(*sem):
    return pltpu.CompilerParams(dimension_semantics=sem, vmem_limit_bytes=VMEM_LIMIT_BYTES)


def _inproj_kernel(x_ref, g_ref, w_ref, wlr_ref, o_ref, lr_ref, h_ref):
    @pl.when(pl.program_id(1) == 0)
    def _():
        x = x_ref[...]
        ms = jnp.mean(x * x, axis=-1, keepdims=True)
        h = (x * lax.rsqrt(ms + NORM_EPS) * g_ref[...]).astype(BF16)
        h_ref[...] = h
        o_ref[...] = jnp.dot(h, w_ref[...], preferred_element_type=F32).astype(o_ref.dtype)
        lr_ref[...] = jnp.dot(h, wlr_ref[...], preferred_element_type=F32)

    @pl.when(pl.program_id(1) > 0)
    def _():
        o_ref[...] = jnp.dot(h_ref[...], w_ref[...], preferred_element_type=F32).astype(o_ref.dtype)


def _inproj(x2, g1, w_main, w_lr):
    m, d = x2.shape
    tm, tn = INPROJ_TM, INPROJ_TN
    return pl.pallas_call(
        _inproj_kernel,
        grid=(m // tm, MAIN_WIDTH // tn),
        in_specs=[
            pl.BlockSpec((tm, d), lambda i, j: (i, 0)),
            pl.BlockSpec((1, d), lambda i, j: (0, 0)),
            pl.BlockSpec((d, tn), lambda i, j: (0, j)),
            pl.BlockSpec((d, LR_PAD), lambda i, j: (0, 0)),
        ],
        out_specs=[
            pl.BlockSpec((tm, tn), lambda i, j: (i, j)),
            pl.BlockSpec((tm, LR_PAD), lambda i, j: (i, 0)),
        ],
        out_shape=[
            jax.ShapeDtypeStruct((m, MAIN_WIDTH), BF16),
            jax.ShapeDtypeStruct((m, LR_PAD), F32),
        ],
        scratch_shapes=[pltpu.VMEM((tm, d), BF16)],
        compiler_params=_cparams("parallel", "arbitrary"),
        name="inproj",
    )(x2, g1, w_main, w_lr)


def _norm_rope(t, gain, cos, sin):
    t = t * lax.rsqrt(jnp.mean(t * t, axis=-1, keepdims=True) + NORM_EPS) * gain
    return t * cos + pltpu.roll(t, HEAD_DIM // 2, 1) * sin


def _attn_kernel(sink_ref, q_ref, k_ref, v_ref, cos_ref, sin_ref, qg_ref, kg_ref, o_ref, kr_ref,
                 *, tq, seq):
    i = pl.program_id(1)

    @pl.when(i == 0)
    def _():
        def body(c, carry):
            r0 = pl.multiple_of(c * ROPE_ROWS, ROPE_ROWS)
            cs = cos_ref[pl.ds(r0, ROPE_ROWS), :]
            sn = sin_ref[pl.ds(r0, ROPE_ROWS), :]
            for h in range(KV_HEADS):
                cols = slice(h * HEAD_DIM, (h + 1) * HEAD_DIM)
                kk = k_ref[0, pl.ds(r0, ROPE_ROWS), cols].astype(F32)
                kr_ref[pl.ds(r0, ROPE_ROWS), cols] = _norm_rope(kk, kg_ref[...], cs, sn).astype(BF16)
            return carry

        lax.fori_loop(0, seq // ROPE_ROWS, body, 0)

    win = tq + 2 * WINDOW
    q0 = pl.multiple_of(i * tq, tq)
    start = pl.multiple_of(jnp.clip(q0 - WINDOW, 0, seq - win), WINDOW)
    cs = cos_ref[pl.ds(q0, tq), :]
    sn = sin_ref[pl.ds(q0, tq), :]
    row = lax.broadcasted_iota(jnp.int32, (tq, win), 0)
    col = lax.broadcasted_iota(jnp.int32, (tq, win), 1)
    mask = (jnp.abs((start + col) - (q0 + row)) <= WINDOW)[None]
    grp = lax.broadcasted_iota(jnp.int32, (GQA_GROUP, 1, 1), 0)
    scale = HEAD_DIM ** -0.5

    for kvh in range(KV_HEADS):
        cols = slice(kvh * HEAD_DIM, (kvh + 1) * HEAD_DIM)
        kw = kr_ref[pl.ds(start, win), cols]
        vw = v_ref[0, pl.ds(start, win), cols]
        qs = []
        sink = jnp.zeros((GQA_GROUP, 1, 1), F32)
        for g in range(GQA_GROUP):
            h = kvh * GQA_GROUP + g
            qq = q_ref[0, :, h * HEAD_DIM:(h + 1) * HEAD_DIM].astype(F32)
            qs.append((_norm_rope(qq, qg_ref[...], cs, sn) * scale).astype(BF16))
            sink = jnp.where(grp == g, sink_ref[h], sink)
        qst = jnp.concatenate(qs, axis=0)
        s = lax.dot_general(qst, kw, (((1,), (1,)), ((), ())), preferred_element_type=F32)
        s = jnp.where(mask, s.reshape(GQA_GROUP, tq, win), -jnp.inf)
        m = jnp.maximum(jnp.max(s, axis=-1, keepdims=True), sink)
        p = jnp.exp(s - m)
        denom = jnp.sum(p, axis=-1, keepdims=True) + jnp.exp(sink - m)
        o = jnp.dot(p.astype(BF16).reshape(GQA_GROUP * tq, win), vw, preferred_element_type=F32)
        o = o.reshape(GQA_GROUP, tq, HEAD_DIM) / denom
        for g in range(GQA_GROUP):
            h = kvh * GQA_GROUP + g
            o_ref[0, :, h * HEAD_DIM:(h + 1) * HEAD_DIM] = o[g].astype(o_ref.dtype)


def _attention(proj3, cos, sin, qg, kg, sink):
    b, seq, _ = proj3.shape
    tq = ATTN_TQ
    kvw = KV_HEADS * HEAD_DIM
    return pl.pallas_call(
        functools.partial(_attn_kernel, tq=tq, seq=seq),
        grid=(b, seq // tq),
        in_specs=[
            pl.BlockSpec(memory_space=pltpu.SMEM),
            pl.BlockSpec((1, tq, ATTN_WIDTH), lambda bi, i: (bi, i, OFF_QA // ATTN_WIDTH)),
            pl.BlockSpec((1, seq, kvw), lambda bi, i: (bi, 0, OFF_KA // kvw)),
            pl.BlockSpec((1, seq, kvw), lambda bi, i: (bi, 0, OFF_VA // kvw)),
            pl.BlockSpec((seq, HEAD_DIM), lambda bi, i: (0, 0)),
            pl.BlockSpec((seq, HEAD_DIM), lambda bi, i: (0, 0)),
            pl.BlockSpec((1, HEAD_DIM), lambda bi, i: (0, 0)),
            pl.BlockSpec((1, HEAD_DIM), lambda bi, i: (0, 0)),
        ],
        out_specs=pl.BlockSpec((1, tq, ATTN_WIDTH), lambda bi, i: (bi, i, 0)),
        out_shape=jax.ShapeDtypeStruct((b, seq, ATTN_WIDTH), BF16),
        scratch_shapes=[pltpu.VMEM((seq, kvw), BF16)],
        compiler_params=_cparams("parallel", "arbitrary"),
        name="swa",
    )(sink, proj3, proj3, proj3, cos, sin, qg, kg)


def _log_sigmoid(z):
    return jnp.minimum(z, 0.0) - jnp.log(1.0 + jnp.exp(-jnp.abs(z)))


def _split2(g):
    hi = g.astype(BF16)
    lo = (g - hi.astype(F32)).astype(BF16)
    return hi, lo


def _gla_kernel(q_ref, k_ref, v_ref, gate_ref, lr_ref, wa_ref, ba_ref, gn_ref,
                o_ref, qin_ref, kv_ref, dec_ref, acc_ref, st_ref, *, seq):
    c = GLA_CHUNK
    n = seq // c
    blk = GLA_BLOCK
    cpb = blk // c

    ri = lax.broadcasted_iota(jnp.int32, (blk, blk), 0)
    ci = lax.broadcasted_iota(jnp.int32, (blk, blk), 1)
    same = (ri // c) == (ci // c)
    lower_incl = same & (ci <= ri)
    upper_strict = same & (ci > ri)
    tri_f = jnp.where(lower_incl, 1.0, 0.0).astype(BF16)
    qscale = GLA_DK ** -0.5

    def intra(t, carry):
        r0 = pl.multiple_of(t * blk, blk)
        lr = lr_ref[0, pl.ds(r0, blk), :].astype(BF16)
        q3 = (q_ref[0, pl.ds(r0, blk), :].astype(F32) * qscale).reshape(cpb, c, GLA_DK)
        k3 = k_ref[0, pl.ds(r0, blk), :].astype(F32).reshape(cpb, c, GLA_DK)
        v = v_ref[0, pl.ds(r0, blk), :]

        z = jnp.dot(lr, wa_ref[...], preferred_element_type=F32) + ba_ref[...]
        g = _log_sigmoid(z) / GLA_GATE_NORMALIZER
        g_hi, g_lo = _split2(g)
        pre = jnp.dot(tri_f, jnp.concatenate([g_hi, g_lo], axis=1), preferred_element_type=F32)
        prefix = pre[:, :2 * GLA_DK] + pre[:, 2 * GLA_DK:]
        bcum_f = prefix[:, :GLA_DK].reshape(cpb, c, GLA_DK)
        pref_b = prefix[:, GLA_DK:].reshape(cpb, c, GLA_DK)
        bcum_b = pref_b[:, c - 1:c, :] - pref_b + g[:, GLA_DK:].reshape(cpb, c, GLA_DK)

        def one_direction(bcum, keep, ref_row, last_row, slot):
            b_mid = bcum[:, ref_row:ref_row + 1, :]
            b_end = bcum[:, last_row:last_row + 1, :]
            qe = q3 * jnp.exp(bcum - b_mid)
            ke = k3 * jnp.exp(b_mid - bcum)
            qin_ref[slot, pl.ds(r0, blk), :] = (qe * jnp.exp(b_mid)).astype(BF16).reshape(blk, GLA_DK)
            kout = (ke * jnp.exp(b_end - b_mid)).astype(BF16)
            for j in range(cpb):
                kv_ref[slot, t * cpb + j] = lax.dot_general(
                    v[j * c:(j + 1) * c], kout[j], (((0,), (0,)), ((), ())), preferred_element_type=F32)
            dec = jnp.broadcast_to(jnp.exp(b_end), (cpb, 8, GLA_DK)).reshape(cpb * 8, GLA_DK)
            dec_ref[slot, pl.ds(pl.multiple_of(t * (cpb * 8), cpb * 8), cpb * 8), :] = dec
            a = lax.dot_general(qe.astype(BF16).reshape(blk, GLA_DK), ke.astype(BF16).reshape(blk, GLA_DK),
                                (((1,), (1,)), ((), ())), preferred_element_type=F32)
            return jnp.where(keep, a, 0.0).astype(BF16)

        a_f = one_direction(bcum_f, lower_incl, c // 2, c - 1, 0)
        a_b = one_direction(bcum_b, upper_strict, c - 1 - c // 2, 0, 1)
        o = jnp.dot(jnp.concatenate([a_f, a_b], axis=1), jnp.concatenate([v, v], axis=0),
                    preferred_element_type=F32)
        acc_ref[pl.ds(r0, blk), :] = o
        return carry

    lax.fori_loop(0, seq // blk, intra, 0, unroll=8)

    def step(idx, slot):
        r0 = pl.multiple_of(idx * c, c)
        st = st_ref[slot]
        o = lax.dot_general(qin_ref[slot, pl.ds(r0, c), :], st.astype(BF16), (((1,), (1,)), ((), ())),
                            preferred_element_type=F32)
        dec = dec_ref[slot, pl.ds(pl.multiple_of(idx * 8, 8), 8), :]
        st_ref[slot] = ((st.reshape(GLA_DV // 8, 8, GLA_DK) * dec[None]).reshape(GLA_DV, GLA_DK)
                        + kv_ref[slot, idx])
        return r0, o

    def accumulate(r0, o):
        acc_ref[pl.ds(r0, c), :] += o

    def finalize(r0, o):
        o = o + acc_ref[pl.ds(r0, c), :]
        y = o * lax.rsqrt(jnp.mean(o * o, axis=-1, keepdims=True) + NORM_EPS) * gn_ref[...]
        gate = gate_ref[0, pl.ds(r0, c), :].astype(F32)
        o_ref[0, pl.ds(r0, c), :] = (y * (gate * jax.nn.sigmoid(gate))).astype(o_ref.dtype)

    st_ref[...] = jnp.zeros_like(st_ref)

    def first_half(t, carry):
        accumulate(*step(t, 0))
        accumulate(*step(n - 1 - t, 1))
        return carry

    def second_half(t, carry):
        finalize(*step(t, 0))
        finalize(*step(n - 1 - t, 1))
        return carry

    lax.fori_loop(0, n // 2, first_half, 0, unroll=4)
    lax.fori_loop(n // 2, n, second_half, 0, unroll=4)


def _gla(proj3, lr3, wa, ba, gn):
    b, seq, _ = proj3.shape
    assert (seq // GLA_CHUNK) % 8 == 0 and seq % GLA_BLOCK == 0
    return pl.pallas_call(
        functools.partial(_gla_kernel, seq=seq),
        grid=(b, GLA_HEADS),
        in_specs=[
            pl.BlockSpec((1, seq, GLA_DK), lambda bi, h: (bi, 0, OFF_QG // GLA_DK + h)),
            pl.BlockSpec((1, seq, GLA_DK), lambda bi, h: (bi, 0, OFF_KG // GLA_DK + h)),
            pl.BlockSpec((1, seq, GLA_DV), lambda bi, h: (bi, 0, OFF_VG // GLA_DV + h)),
            pl.BlockSpec((1, seq, GLA_DV), lambda bi, h: (bi, 0, OFF_GG // GLA_DV + h)),
            pl.BlockSpec((1, seq, LR_PAD), lambda bi, h: (bi, 0, 0)),
            pl.BlockSpec((LR_PAD, 2 * GLA_DK), lambda bi, h: (0, h)),
            pl.BlockSpec((1, 2 * GLA_DK), lambda bi, h: (0, h)),
            pl.BlockSpec((1, GLA_DV), lambda bi, h: (0, 0)),
        ],
        out_specs=pl.BlockSpec((1, seq, GLA_DV), lambda bi, h: (bi, 0, h)),
        out_shape=jax.ShapeDtypeStruct((b, seq, GLA_WIDTH), BF16),
        scratch_shapes=[
            pltpu.VMEM((2, seq, GLA_DK), BF16),
            pltpu.VMEM((2, seq // GLA_CHUNK, GLA_DV, GLA_DK), F32),
            pltpu.VMEM((2, seq // GLA_CHUNK * 8, GLA_DK), F32),
            pltpu.VMEM((seq, GLA_DV), F32),
            pltpu.VMEM((2, GLA_DV, GLA_DK), F32),
        ],
        compiler_params=_cparams("parallel", "arbitrary"),
        name="gla",
    )(proj3, proj3, proj3, proj3, lr3, wa, ba, gn)


def _outproj_kernel(oa_ref, og_ref, x_ref, w_ref, g_ref, x1_ref, h2_ref):
    wa = oa_ref.shape[1]
    acc = jnp.dot(oa_ref[...], w_ref[:wa, :], preferred_element_type=F32)
    acc = acc + jnp.dot(og_ref[...], w_ref[wa:, :], preferred_element_type=F32)
    x1 = x_ref[...] + acc
    x1_ref[...] = x1
    ms = jnp.mean(x1 * x1, axis=-1, keepdims=True)
    h2_ref[...] = (x1 * lax.rsqrt(ms + NORM_EPS) * g_ref[...]).astype(h2_ref.dtype)


def _outproj(oa, og, x2, w_out, g2):
    m, d = x2.shape
    tm = OUTPROJ_TM
    wa, wg = oa.shape[1], og.shape[1]
    return pl.pallas_call(
        _outproj_kernel,
        grid=(m // tm,),
        in_specs=[
            pl.BlockSpec((tm, wa), lambda i: (i, 0)),
            pl.BlockSpec((tm, wg), lambda i: (i, 0)),
            pl.BlockSpec((tm, d), lambda i: (i, 0)),
            pl.BlockSpec((wa + wg, d), lambda i: (0, 0)),
            pl.BlockSpec((1, d), lambda i: (0, 0)),
        ],
        out_specs=[
            pl.BlockSpec((tm, d), lambda i: (i, 0)),
            pl.BlockSpec((tm, d), lambda i: (i, 0)),
        ],
        out_shape=[
            jax.ShapeDtypeStruct((m, d), F32),
            jax.ShapeDtypeStruct((m, d), BF16),
        ],
        compiler_params=_cparams("parallel"),
        name="outproj",
    )(oa, og, x2, w_out, g2)


def _ffn_kernel(h_ref, hp_ref, hn_ref, x1_ref, wg_ref, wv_ref, cwg_ref, cwv_ref, cbg_ref, cbv_ref,
                wd_ref, o_ref, hext_ref, *, tm, tiles_per_seq):
    i = pl.program_id(0)
    f = pl.program_id(1)

    th = tm // 2
    rows = th + HALO

    @pl.when(f == 0)
    def _():
        pos = i % tiles_per_seq
        r = lax.broadcasted_iota(jnp.int32, (HALO, 1), 0)
        zero = jnp.zeros_like(hn_ref[...])
        prev0 = jnp.where((r == HALO - 1) & (pos != 0), hp_ref[...], zero)
        next1 = jnp.where((r == 0) & (pos != tiles_per_seq - 1), hn_ref[...], zero)
        hext_ref[0, :th, :] = h_ref[:th, :]
        hext_ref[0, th:, :] = jnp.where(r == 0, h_ref[th:th + HALO, :], prev0)
        hext_ref[1, :th, :] = h_ref[th:, :]
        hext_ref[1, th:, :] = jnp.where(r == HALO - 1, h_ref[th - HALO:th, :], next1)
        o_ref[...] = jnp.zeros_like(o_ref)

    def conv(u, cw_ref, cb_ref):
        prev = pltpu.roll(u, 1, 0)
        nxt = pltpu.roll(u, rows - 1, 0)
        y = prev * cw_ref[0:1, :] + u * cw_ref[1:2, :] + nxt * cw_ref[2:3, :] + cb_ref[...]
        return y[:th]

    acts = []
    for half in range(2):
        hext = hext_ref[half]
        ug = conv(jnp.dot(hext, wg_ref[...], preferred_element_type=F32), cwg_ref, cbg_ref)
        uv = conv(jnp.dot(hext, wv_ref[...], preferred_element_type=F32), cwv_ref, cbv_ref)
        acts.append((ug * jax.nn.sigmoid(ug) * uv).astype(BF16))
    for half in range(2):
        o_ref[half * th:(half + 1) * th, :] += jnp.dot(acts[half], wd_ref[...], preferred_element_type=F32)

    @pl.when(f < tm // FFN_XROWS)
    def _():
        r0 = pl.multiple_of(f * FFN_XROWS, FFN_XROWS)
        o_ref[pl.ds(r0, FFN_XROWS), :] += x1_ref[...]


def _ffn(h2, x1, w_up, conv_w, conv_b, w_down, seq):
    m, d = h2.shape
    dff = w_down.shape[0]
    tm, tf = FFN_TM, FFN_TF
    nf = dff // tf
    hb = tm // HALO
    last_halo = m // HALO - 1
    xch = tm // FFN_XROWS
    assert nf >= xch
    return pl.pallas_call(
        functools.partial(_ffn_kernel, tm=tm, tiles_per_seq=seq // tm),
        grid=(m // tm, nf),
        in_specs=[
            pl.BlockSpec((tm, d), lambda i, f: (i, 0)),
            pl.BlockSpec((HALO, d), lambda i, f: (jnp.maximum(i * hb - 1, 0), 0)),
            pl.BlockSpec((HALO, d), lambda i, f: (jnp.minimum((i + 1) * hb, last_halo), 0)),
            pl.BlockSpec((FFN_XROWS, d), lambda i, f: (i * xch + jnp.minimum(f, xch - 1), 0)),
            pl.BlockSpec((d, tf), lambda i, f: (0, f)),
            pl.BlockSpec((d, tf), lambda i, f: (0, nf + f)),
            pl.BlockSpec((3, tf), lambda i, f: (0, f)),
            pl.BlockSpec((3, tf), lambda i, f: (0, nf + f)),
            pl.BlockSpec((1, tf), lambda i, f: (0, f)),
            pl.BlockSpec((1, tf), lambda i, f: (0, nf + f)),
            pl.BlockSpec((tf, d), lambda i, f: (f, 0)),
        ],
        out_specs=pl.BlockSpec((tm, d), lambda i, f: (i, 0)),
        out_shape=jax.ShapeDtypeStruct((m, d), F32),
        scratch_shapes=[
            pltpu.VMEM((2, tm // 2 + HALO, d), BF16),
        ],
        compiler_params=_cparams("parallel", "arbitrary"),
        name="convffn",
    )(h2, h2, h2, x1, w_up, w_up, conv_w, conv_w, conv_b, conv_b, w_down)


def _rope_tables(seq):
    half = HEAD_DIM // 2
    inv = 1.0 / (ROPE_THETA ** (np.arange(half, dtype=np.float64) / half))
    ang = np.arange(seq, dtype=np.float64)[:, None] * inv[None, :]
    cos = np.cos(ang).astype(np.float32)
    sin = np.sin(ang).astype(np.float32)
    return (jnp.asarray(np.concatenate([cos, cos], axis=-1)),
            jnp.asarray(np.concatenate([-sin, sin], axis=-1)))


def _decay_params(wa2_f, ba_f, wa2_b, ba_b):
    r = GLA_DECAY_RANK
    wf = jnp.pad(wa2_f, ((0, LR_PAD - r), (0, 0))).reshape(LR_PAD, GLA_HEADS, 1, GLA_DK)
    wb = jnp.pad(wa2_b, ((r, LR_PAD - 2 * r), (0, 0))).reshape(LR_PAD, GLA_HEADS, 1, GLA_DK)
    wa = jnp.concatenate([wf, wb], axis=2).reshape(LR_PAD, 2 * GLA_KEY_WIDTH).astype(BF16)
    ba = jnp.concatenate([ba_f.reshape(GLA_HEADS, 1, GLA_DK), ba_b.reshape(GLA_HEADS, 1, GLA_DK)],
                         axis=1).reshape(1, 2 * GLA_KEY_WIDTH)
    return wa, ba


def _layer(x, norm1_g, w_in, q_norm_g, k_norm_g, sink, wa2_f, ba_f, wa2_b, ba_b, gla_norm_g,
           w_out, norm2_g, w_up, conv_w, conv_b, w_down):
    b, seq, d = x.shape
    m = b * seq
    x2 = x.reshape(m, d)
    w_lr = jnp.pad(w_in[:, MAIN_WIDTH:], ((0, 0), (0, LR_PAD - 2 * GLA_DECAY_RANK))).astype(BF16)
    proj, lr = _inproj(x2, norm1_g.reshape(1, d), w_in.astype(BF16), w_lr)
    proj3 = proj.reshape(b, seq, MAIN_WIDTH)
    cos, sin = _rope_tables(seq)
    o_attn = _attention(proj3, cos, sin, q_norm_g.reshape(1, HEAD_DIM), k_norm_g.reshape(1, HEAD_DIM), sink)
    wa, ba = _decay_params(wa2_f, ba_f, wa2_b, ba_b)
    o_gla = _gla(proj3, lr.reshape(b, seq, LR_PAD), wa, ba, gla_norm_g.reshape(1, GLA_DV))
    x1, h2 = _outproj(o_attn.reshape(m, ATTN_WIDTH), o_gla.reshape(m, GLA_WIDTH), x2,
                      w_out.astype(BF16), norm2_g.reshape(1, d))
    out = _ffn(h2, x1, w_up.astype(BF16), conv_w, conv_b.reshape(1, -1), w_down.astype(BF16), seq)
    return out.reshape(b, seq, d)


def kernel(x, norm1_g, w_in, attn_q_norm_g, attn_k_norm_g, attn_sink, gla_wa2_fwd, gla_ba_fwd, gla_wa2_bwd, gla_ba_bwd, gla_out_norm_g, w_out, norm2_g, w_up, conv_w, conv_b, w_down):
    for l in range(w_in.shape[0]):
        x = _layer(x, norm1_g[l], w_in[l], attn_q_norm_g[l], attn_k_norm_g[l], attn_sink[l],
                   gla_wa2_fwd[l], gla_ba_fwd[l], gla_wa2_bwd[l], gla_ba_bwd[l], gla_out_norm_g[l],
                   w_out[l], norm2_g[l], w_up[l], conv_w[l], conv_b[l], w_down[l])
    return x
```

```python
import functools

import jax
import jax.numpy as jnp
import numpy as np
from jax import lax
from jax.experimental import pallas as pl
from jax.experimental.pallas import tpu as pltpu

F32 = jnp.float32
BF16 = jnp.bfloat16

HEAD_DIM = 128
KV_HEADS = 2
GQA_GROUP = 4
ATTN_HEADS = KV_HEADS * GQA_GROUP
ATTN_WIDTH = ATTN_HEADS * HEAD_DIM
WINDOW = 128
ROPE_THETA = 10000.0
GLA_HEADS = 4
GLA_DK = 128
GLA_DV = 256
GLA_KEY_WIDTH = GLA_HEADS * GLA_DK
GLA_WIDTH = GLA_HEADS * GLA_DV
GLA_DECAY_RANK = 16
GLA_GATE_NORMALIZER = 16.0
GLA_CHUNK = 64
GLA_BLOCK = 256
NORM_EPS = 1e-6

OFF_QA = 0
OFF_KA = OFF_QA + ATTN_WIDTH
OFF_VA = OFF_KA + KV_HEADS * HEAD_DIM
OFF_QG = OFF_VA + KV_HEADS * HEAD_DIM
OFF_KG = OFF_QG + GLA_KEY_WIDTH
OFF_VG = OFF_KG + GLA_KEY_WIDTH
OFF_GG = OFF_VG + GLA_WIDTH
OFF_LR = OFF_GG + GLA_WIDTH
MAIN_WIDTH = OFF_LR
LR_PAD = 128

VMEM_LIMIT_BYTES = 56 * 1024 * 1024
INPROJ_TM = 1024
INPROJ_TN = 1536
ATTN_TQ = 128
ROPE_ROWS = 512
OUTPROJ_TM = 512
FFN_TM = 1024
FFN_TF = 512
FFN_XROWS = 128
HALO = 16


def _cparams(*sem):
    return pltpu.CompilerParams(dimension_semantics=sem, vmem_limit_bytes=VMEM_LIMIT_BYTES)


def _inproj_kernel(x_ref, g_ref, w_ref, wlr_ref, o_ref, lr_ref, h_ref):
    @pl.when(pl.program_id(1) == 0)
    def _():
        x = x_ref[...]
        ms = jnp.mean(x * x, axis=-1, keepdims=True)
        h = (x * lax.rsqrt(ms + NORM_EPS) * g_ref[...]).astype(BF16)
        h_ref[...] = h
        o_ref[...] = jnp.dot(h, w_ref[...], preferred_element_type=F32).astype(o_ref.dtype)
        lr_ref[...] = jnp.dot(h, wlr_ref[...], preferred_element_type=F32)

    @pl.when(pl.program_id(1) > 0)
    def _():
        o_ref[...] = jnp.dot(h_ref[...], w_ref[...], preferred_element_type=F32).astype(o_ref.dtype)


def _inproj(x2, g1, w_main, w_lr):
    m, d = x2.shape
    tm, tn = INPROJ_TM, INPROJ_TN
    assert m % tm == 0 and MAIN_WIDTH % tn == 0 and w_main.shape == (d, MAIN_WIDTH + 2 * GLA_DECAY_RANK)
    return pl.pallas_call(
        _inproj_kernel,
        grid=(m // tm, MAIN_WIDTH // tn),
        in_specs=[
            pl.BlockSpec((tm, d), lambda i, j: (i, 0)),
            pl.BlockSpec((1, d), lambda i, j: (0, 0)),
            pl.BlockSpec((d, tn), lambda i, j: (0, j)),
            pl.BlockSpec((d, LR_PAD), lambda i, j: (0, 0)),
        ],
        out_specs=[
            pl.BlockSpec((tm, tn), lambda i, j: (i, j)),
            pl.BlockSpec((tm, LR_PAD), lambda i, j: (i, 0)),
        ],
        out_shape=[
            jax.ShapeDtypeStruct((m, MAIN_WIDTH), BF16),
            jax.ShapeDtypeStruct((m, LR_PAD), F32),
        ],
        scratch_shapes=[pltpu.VMEM((tm, d), BF16)],
        compiler_params=_cparams("parallel", "arbitrary"),
        name="inproj",
    )(x2, g1, w_main, w_lr)


def _norm_rope(t, gain, cos, sin):
    t = t * lax.rsqrt(jnp.mean(t * t, axis=-1, keepdims=True) + NORM_EPS) * gain
    return t * cos + pltpu.roll(t, HEAD_DIM // 2, 1) * sin


def _attn_kernel(sink_ref, q_ref, k_ref, v_ref, cos_ref, sin_ref, qg_ref, kg_ref, o_ref, kr_ref,
                 *, tq, seq):
    i = pl.program_id(1)

    @pl.when(i == 0)
    def _():
        def body(c, carry):
            r0 = pl.multiple_of(c * ROPE_ROWS, ROPE_ROWS)
            cs = cos_ref[pl.ds(r0, ROPE_ROWS), :]
            sn = sin_ref[pl.ds(r0, ROPE_ROWS), :]
            for h in range(KV_HEADS):
                cols = slice(h * HEAD_DIM, (h + 1) * HEAD_DIM)
                kk = k_ref[0, pl.ds(r0, ROPE_ROWS), cols].astype(F32)
                kr_ref[pl.ds(r0, ROPE_ROWS), cols] = _norm_rope(kk, kg_ref[...], cs, sn).astype(BF16)
            return carry

        lax.fori_loop(0, seq // ROPE_ROWS, body, 0)

    win = tq + 2 * WINDOW
    q0 = pl.multiple_of(i * tq, tq)
    start = pl.multiple_of(jnp.clip(q0 - WINDOW, 0, seq - win), WINDOW)
    cs = cos_ref[pl.ds(q0, tq), :]
    sn = sin_ref[pl.ds(q0, tq), :]
    row = lax.broadcasted_iota(jnp.int32, (tq, win), 0)
    col = lax.broadcasted_iota(jnp.int32, (tq, win), 1)
    mask = (jnp.abs((start + col) - (q0 + row)) <= WINDOW)[None]
    grp = lax.broadcasted_iota(jnp.int32, (GQA_GROUP, 1, 1), 0)
    scale = HEAD_DIM ** -0.5

    for kvh in range(KV_HEADS):
        cols = slice(kvh * HEAD_DIM, (kvh + 1) * HEAD_DIM)
        kw = kr_ref[pl.ds(start, win), cols]
        vw = v_ref[0, pl.ds(start, win), cols]
        qs = []
        sink = jnp.zeros((GQA_GROUP, 1, 1), F32)
        for g in range(GQA_GROUP):
            h = kvh * GQA_GROUP + g
            qq = q_ref[0, :, h * HEAD_DIM:(h + 1) * HEAD_DIM].astype(F32)
            qs.append((_norm_rope(qq, qg_ref[...], cs, sn) * scale).astype(BF16))
            sink = jnp.where(grp == g, sink_ref[h], sink)
        qst = jnp.concatenate(qs, axis=0)
        s = lax.dot_general(qst, kw, (((1,), (1,)), ((), ())), preferred_element_type=F32)
        s = jnp.where(mask, s.reshape(GQA_GROUP, tq, win), -jnp.inf)
        m = jnp.maximum(jnp.max(s, axis=-1, keepdims=True), sink)
        p = jnp.exp(s - m)
        denom = jnp.sum(p, axis=-1, keepdims=True) + jnp.exp(sink - m)
        o = jnp.dot(p.astype(BF16).reshape(GQA_GROUP * tq, win), vw, preferred_element_type=F32)
        o = o.reshape(GQA_GROUP, tq, HEAD_DIM) / denom
        for g in range(GQA_GROUP):
            h = kvh * GQA_GROUP + g
            o_ref[0, :, h * HEAD_DIM:(h + 1) * HEAD_DIM] = o[g].astype(o_ref.dtype)


def _attention(proj3, cos, sin, qg, kg, sink):
    b, seq, _ = proj3.shape
    tq = ATTN_TQ
    kvw = KV_HEADS * HEAD_DIM
    assert seq % tq == 0 and seq % ROPE_ROWS == 0 and seq >= tq + 2 * WINDOW and tq % WINDOW == 0
    return pl.pallas_call(
        functools.partial(_attn_kernel, tq=tq, seq=seq),
        grid=(b, seq // tq),
        in_specs=[
            pl.BlockSpec(memory_space=pltpu.SMEM),
            pl.BlockSpec((1, tq, ATTN_WIDTH), lambda bi, i: (bi, i, OFF_QA // ATTN_WIDTH)),
            pl.BlockSpec((1, seq, kvw), lambda bi, i: (bi, 0, OFF_KA // kvw)),
            pl.BlockSpec((1, seq, kvw), lambda bi, i: (bi, 0, OFF_VA // kvw)),
            pl.BlockSpec((seq, HEAD_DIM), lambda bi, i: (0, 0)),
            pl.BlockSpec((seq, HEAD_DIM), lambda bi, i: (0, 0)),
            pl.BlockSpec((1, HEAD_DIM), lambda bi, i: (0, 0)),
            pl.BlockSpec((1, HEAD_DIM), lambda bi, i: (0, 0)),
        ],
        out_specs=pl.BlockSpec((1, tq, ATTN_WIDTH), lambda bi, i: (bi, i, 0)),
        out_shape=jax.ShapeDtypeStruct((b, seq, ATTN_WIDTH), BF16),
        scratch_shapes=[pltpu.VMEM((seq, kvw), BF16)],
        compiler_params=_cparams("parallel", "arbitrary"),
        name="swa",
    )(sink, proj3, proj3, proj3, cos, sin, qg, kg)


def _log_sigmoid(z):
    return jnp.minimum(z, 0.0) - jnp.log(1.0 + jnp.exp(-jnp.abs(z)))


def _split2(g):
    hi = g.astype(BF16)
    lo = (g - hi.astype(F32)).astype(BF16)
    return hi, lo


def _gla_kernel(q_ref, k_ref, v_ref, gate_ref, lr_ref, wa_ref, ba_ref, gn_ref,
                o_ref, qin_ref, kv_ref, dec_ref, acc_ref, st_ref, *, seq):
    c = GLA_CHUNK
    n = seq // c
    blk = GLA_BLOCK
    cpb = blk // c

    ri = lax.broadcasted_iota(jnp.int32, (blk, blk), 0)
    ci = lax.broadcasted_iota(jnp.int32, (blk, blk), 1)
    same = (ri // c) == (ci // c)
    lower_incl = same & (ci <= ri)
    upper_strict = same & (ci > ri)
    tri_f = jnp.where(lower_incl, 1.0, 0.0).astype(BF16)
    qscale = GLA_DK ** -0.5

    def intra(t, carry):
        r0 = pl.multiple_of(t * blk, blk)
        lr = lr_ref[0, pl.ds(r0, blk), :].astype(BF16)
        q3 = (q_ref[0, pl.ds(r0, blk), :].astype(F32) * qscale).reshape(cpb, c, GLA_DK)
        k3 = k_ref[0, pl.ds(r0, blk), :].astype(F32).reshape(cpb, c, GLA_DK)
        v = v_ref[0, pl.ds(r0, blk), :]

        z = jnp.dot(lr, wa_ref[...], preferred_element_type=F32) + ba_ref[...]
        g = _log_sigmoid(z) / GLA_GATE_NORMALIZER
        g_hi, g_lo = _split2(g)
        pre = jnp.dot(tri_f, jnp.concatenate([g_hi, g_lo], axis=1), preferred_element_type=F32)
        prefix = pre[:, :2 * GLA_DK] + pre[:, 2 * GLA_DK:]
        bcum_f = prefix[:, :GLA_DK].reshape(cpb, c, GLA_DK)
        pref_b = prefix[:, GLA_DK:].reshape(cpb, c, GLA_DK)
        bcum_b = pref_b[:, c - 1:c, :] - pref_b + g[:, GLA_DK:].reshape(cpb, c, GLA_DK)

        def one_direction(bcum, keep, ref_row, last_row, slot):
            b_mid = bcum[:, ref_row:ref_row + 1, :]
            b_end = bcum[:, last_row:last_row + 1, :]
            qe = q3 * jnp.exp(bcum - b_mid)
            ke = k3 * jnp.exp(b_mid - bcum)
            qin_ref[slot, pl.ds(r0, blk), :] = (qe * jnp.exp(b_mid)).astype(BF16).reshape(blk, GLA_DK)
            kout = (ke * jnp.exp(b_end - b_mid)).astype(BF16)
            for j in range(cpb):
                kv_ref[slot, t * cpb + j] = lax.dot_general(
                    v[j * c:(j + 1) * c], kout[j], (((0,), (0,)), ((), ())), preferred_element_type=F32)
            dec = jnp.broadcast_to(jnp.exp(b_end), (cpb, 8, GLA_DK)).reshape(cpb * 8, GLA_DK)
            dec_ref[slot, pl.ds(pl.multiple_of(t * (cpb * 8), cpb * 8), cpb * 8), :] = dec
            a = lax.dot_general(qe.astype(BF16).reshape(blk, GLA_DK), ke.astype(BF16).reshape(blk, GLA_DK),
                                (((1,), (1,)), ((), ())), preferred_element_type=F32)
            return jnp.where(keep, a, 0.0).astype(BF16)

        a_f = one_direction(bcum_f, lower_incl, c // 2, c - 1, 0)
        a_b = one_direction(bcum_b, upper_strict, c - 1 - c // 2, 0, 1)
        o = jnp.dot(jnp.concatenate([a_f, a_b], axis=1), jnp.concatenate([v, v], axis=0),
                    preferred_element_type=F32)
        acc_ref[pl.ds(r0, blk), :] = o
        return carry

    lax.fori_loop(0, seq // blk, intra, 0, unroll=8)

    def step(idx, slot):
        r0 = pl.multiple_of(idx * c, c)
        st = st_ref[slot]
        o = lax.dot_general(qin_ref[slot, pl.ds(r0, c), :], st.astype(BF16), (((1,), (1,)), ((), ())),
                            preferred_element_type=F32)
        dec = dec_ref[slot, pl.ds(pl.multiple_of(idx * 8, 8), 8), :]
        st_ref[slot] = ((st.reshape(GLA_DV // 8, 8, GLA_DK) * dec[None]).reshape(GLA_DV, GLA_DK)
                        + kv_ref[slot, idx])
        return r0, o

    def accumulate(r0, o):
        acc_ref[pl.ds(r0, c), :] += o

    def finalize(r0, o):
        o = o + acc_ref[pl.ds(r0, c), :]
        y = o * lax.rsqrt(jnp.mean(o * o, axis=-1, keepdims=True) + NORM_EPS) * gn_ref[...]
        gate = gate_ref[0, pl.ds(r0, c), :].astype(F32)
        o_ref[0, pl.ds(r0, c), :] = (y * (gate * jax.nn.sigmoid(gate))).astype(o_ref.dtype)

    st_ref[...] = jnp.zeros_like(st_ref)

    def first_half(t, carry):
        accumulate(*step(t, 0))
        accumulate(*step(n - 1 - t, 1))
        return carry

    def second_half(t, carry):
        finalize(*step(t, 0))
        finalize(*step(n - 1 - t, 1))
        return carry

    lax.fori_loop(0, n // 2, first_half, 0, unroll=4)
    lax.fori_loop(n // 2, n, second_half, 0, unroll=4)


def _gla(proj3, lr3, wa, ba, gn):
    b, seq, _ = proj3.shape
    assert (seq // GLA_CHUNK) % 8 == 0 and seq % GLA_BLOCK == 0
    return pl.pallas_call(
        functools.partial(_gla_kernel, seq=seq),
        grid=(b, GLA_HEADS),
        in_specs=[
            pl.BlockSpec((1, seq, GLA_DK), lambda bi, h: (bi, 0, OFF_QG // GLA_DK + h)),
            pl.BlockSpec((1, seq, GLA_DK), lambda bi, h: (bi, 0, OFF_KG // GLA_DK + h)),
            pl.BlockSpec((1, seq, GLA_DV), lambda bi, h: (bi, 0, OFF_VG // GLA_DV + h)),
            pl.BlockSpec((1, seq, GLA_DV), lambda bi, h: (bi, 0, OFF_GG // GLA_DV + h)),
            pl.BlockSpec((1, seq, LR_PAD), lambda bi, h: (bi, 0, 0)),
            pl.BlockSpec((LR_PAD, 2 * GLA_DK), lambda bi, h: (0, h)),
            pl.BlockSpec((1, 2 * GLA_DK), lambda bi, h: (0, h)),
            pl.BlockSpec((1, GLA_DV), lambda bi, h: (0, 0)),
        ],
        out_specs=pl.BlockSpec((1, seq, GLA_DV), lambda bi, h: (bi, 0, h)),
        out_shape=jax.ShapeDtypeStruct((b, seq, GLA_WIDTH), BF16),
        scratch_shapes=[
            pltpu.VMEM((2, seq, GLA_DK), BF16),
            pltpu.VMEM((2, seq // GLA_CHUNK, GLA_DV, GLA_DK), F32),
            pltpu.VMEM((2, seq // GLA_CHUNK * 8, GLA_DK), F32),
            pltpu.VMEM((seq, GLA_DV), F32),
            pltpu.VMEM((2, GLA_DV, GLA_DK), F32),
        ],
        compiler_params=_cparams("parallel", "arbitrary"),
        name="gla",
    )(proj3, proj3, proj3, proj3, lr3, wa, ba, gn)


def _outproj_kernel(oa_ref, og_ref, x_ref, w_ref, g_ref, x1_ref, h2_ref):
    wa = oa_ref.shape[1]
    acc = jnp.dot(oa_ref[...], w_ref[:wa, :], preferred_element_type=F32)
    acc = acc + jnp.dot(og_ref[...], w_ref[wa:, :], preferred_element_type=F32)
    x1 = x_ref[...] + acc
    x1_ref[...] = x1
    ms = jnp.mean(x1 * x1, axis=-1, keepdims=True)
    h2_ref[...] = (x1 * lax.rsqrt(ms + NORM_EPS) * g_ref[...]).astype(h2_ref.dtype)


def _outproj(oa, og, x2, w_out, g2):
    m, d = x2.shape
    tm = OUTPROJ_TM
    wa, wg = oa.shape[1], og.shape[1]
    assert m % tm == 0 and w_out.shape == (wa + wg, d)
    return pl.pallas_call(
        _outproj_kernel,
        grid=(m // tm,),
        in_specs=[
            pl.BlockSpec((tm, wa), lambda i: (i, 0)),
            pl.BlockSpec((tm, wg), lambda i: (i, 0)),
            pl.BlockSpec((tm, d), lambda i: (i, 0)),
            pl.BlockSpec((wa + wg, d), lambda i: (0, 0)),
            pl.BlockSpec((1, d), lambda i: (0, 0)),
        ],
        out_specs=[
            pl.BlockSpec((tm, d), lambda i: (i, 0)),
            pl.BlockSpec((tm, d), lambda i: (i, 0)),
        ],
        out_shape=[
            jax.ShapeDtypeStruct((m, d), F32),
            jax.ShapeDtypeStruct((m, d), BF16),
        ],
        compiler_params=_cparams("parallel"),
        name="outproj",
    )(oa, og, x2, w_out, g2)


def _ffn_kernel(h_ref, hp_ref, hn_ref, x1_ref, wg_ref, wv_ref, cwg_ref, cwv_ref, cbg_ref, cbv_ref,
                wd_ref, o_ref, hext_ref, *, tm, tiles_per_seq):
    i = pl.program_id(0)
    f = pl.program_id(1)

    th = tm // 2
    rows = th + HALO

    @pl.when(f == 0)
    def _():
        pos = i % tiles_per_seq
        r = lax.broadcasted_iota(jnp.int32, (HALO, 1), 0)
        zero = jnp.zeros_like(hn_ref[...])
        prev0 = jnp.where((r == HALO - 1) & (pos != 0), hp_ref[...], zero)
        next1 = jnp.where((r == 0) & (pos != tiles_per_seq - 1), hn_ref[...], zero)
        hext_ref[0, :th, :] = h_ref[:th, :]
        hext_ref[0, th:, :] = jnp.where(r == 0, h_ref[th:th + HALO, :], prev0)
        hext_ref[1, :th, :] = h_ref[th:, :]
        hext_ref[1, th:, :] = jnp.where(r == HALO - 1, h_ref[th - HALO:th, :], next1)
        o_ref[...] = jnp.zeros_like(o_ref)

    def conv(u, cw_ref, cb_ref):
        prev = pltpu.roll(u, 1, 0)
        nxt = pltpu.roll(u, rows - 1, 0)
        y = prev * cw_ref[0:1, :] + u * cw_ref[1:2, :] + nxt * cw_ref[2:3, :] + cb_ref[...]
        return y[:th]

    acts = []
    for half in range(2):
        hext = hext_ref[half]
        ug = conv(jnp.dot(hext, wg_ref[...], preferred_element_type=F32), cwg_ref, cbg_ref)
        uv = conv(jnp.dot(hext, wv_ref[...], preferred_element_type=F32), cwv_ref, cbv_ref)
        acts.append((ug * jax.nn.sigmoid(ug) * uv).astype(BF16))
    for half in range(2):
        o_ref[half * th:(half + 1) * th, :] += jnp.dot(acts[half], wd_ref[...], preferred_element_type=F32)

    @pl.when(f < tm // FFN_XROWS)
    def _():
        r0 = pl.multiple_of(f * FFN_XROWS, FFN_XROWS)
        o_ref[pl.ds(r0, FFN_XROWS), :] += x1_ref[...]


def _ffn(h2, x1, w_up, conv_w, conv_b, w_down, seq):
    m, d = h2.shape
    dff = w_down.shape[0]
    tm, tf = FFN_TM, FFN_TF
    nf = dff // tf
    hb = tm // HALO
    last_halo = m // HALO - 1
    xch = tm // FFN_XROWS
    assert nf >= xch and dff % tf == 0 and seq % tm == 0 and m % seq == 0 and w_up.shape == (d, 2 * dff)
    return pl.pallas_call(
        functools.partial(_ffn_kernel, tm=tm, tiles_per_seq=seq // tm),
        grid=(m // tm, nf),
        in_specs=[
            pl.BlockSpec((tm, d), lambda i, f: (i, 0)),
            pl.BlockSpec((HALO, d), lambda i, f: (jnp.maximum(i * hb - 1, 0), 0)),
            pl.BlockSpec((HALO, d), lambda i, f: (jnp.minimum((i + 1) * hb, last_halo), 0)),
            pl.BlockSpec((FFN_XROWS, d), lambda i, f: (i * xch + jnp.minimum(f, xch - 1), 0)),
            pl.BlockSpec((d, tf), lambda i, f: (0, f)),
            pl.BlockSpec((d, tf), lambda i, f: (0, nf + f)),
            pl.BlockSpec((3, tf), lambda i, f: (0, f)),
            pl.BlockSpec((3, tf), lambda i, f: (0, nf + f)),
            pl.BlockSpec((1, tf), lambda i, f: (0, f)),
            pl.BlockSpec((1, tf), lambda i, f: (0, nf + f)),
            pl.BlockSpec((tf, d), lambda i, f: (f, 0)),
        ],
        out_specs=pl.BlockSpec((tm, d), lambda i, f: (i, 0)),
        out_shape=jax.ShapeDtypeStruct((m, d), F32),
        scratch_shapes=[
            pltpu.VMEM((2, tm // 2 + HALO, d), BF16),
        ],
        compiler_params=_cparams("parallel", "arbitrary"),
        name="convffn",
    )(h2, h2, h2, x1, w_up, w_up, conv_w, conv_w, conv_b, conv_b, w_down)


def _rope_tables(seq):
    half = HEAD_DIM // 2
    inv = 1.0 / (ROPE_THETA ** (np.arange(half, dtype=np.float64) / half))
    ang = np.arange(seq, dtype=np.float64)[:, None] * inv[None, :]
    cos = np.cos(ang).astype(np.float32)
    sin = np.sin(ang).astype(np.float32)
    return (jnp.asarray(np.concatenate([cos, cos], axis=-1)),
            jnp.asarray(np.concatenate([-sin, sin], axis=-1)))


def _decay_params(wa2_f, ba_f, wa2_b, ba_b):
    r = GLA_DECAY_RANK
    wf = jnp.pad(wa2_f, ((0, LR_PAD - r), (0, 0))).reshape(LR_PAD, GLA_HEADS, 1, GLA_DK)
    wb = jnp.pad(wa2_b, ((r, LR_PAD - 2 * r), (0, 0))).reshape(LR_PAD, GLA_HEADS, 1, GLA_DK)
    wa = jnp.concatenate([wf, wb], axis=2).reshape(LR_PAD, 2 * GLA_KEY_WIDTH).astype(BF16)
    ba = jnp.concatenate([ba_f.reshape(GLA_HEADS, 1, GLA_DK), ba_b.reshape(GLA_HEADS, 1, GLA_DK)],
                         axis=1).reshape(1, 2 * GLA_KEY_WIDTH)
    return wa, ba


def _layer(x, norm1_g, w_in, q_norm_g, k_norm_g, sink, wa2_f, ba_f, wa2_b, ba_b, gla_norm_g,
           w_out, norm2_g, w_up, conv_w, conv_b, w_down):
    b, seq, d = x.shape
    m = b * seq
    x2 = x.reshape(m, d)
    w_lr = jnp.pad(w_in[:, MAIN_WIDTH:], ((0, 0), (0, LR_PAD - 2 * GLA_DECAY_RANK))).astype(BF16)
    proj, lr = _inproj(x2, norm1_g.reshape(1, d), w_in.astype(BF16), w_lr)
    proj3 = proj.reshape(b, seq, MAIN_WIDTH)
    cos, sin = _rope_tables(seq)
    o_attn = _attention(proj3, cos, sin, q_norm_g.reshape(1, HEAD_DIM), k_norm_g.reshape(1, HEAD_DIM), sink)
    wa, ba = _decay_params(wa2_f, ba_f, wa2_b, ba_b)
    o_gla = _gla(proj3, lr.reshape(b, seq, LR_PAD), wa, ba, gla_norm_g.reshape(1, GLA_DV))
    x1, h2 = _outproj(o_attn.reshape(m, ATTN_WIDTH), o_gla.reshape(m, GLA_WIDTH), x2,
                      w_out.astype(BF16), norm2_g.reshape(1, d))
    out = _ffn(h2, x1, w_up.astype(BF16), conv_w, conv_b.reshape(1, -1), w_down.astype(BF16), seq)
    return out.reshape(b, seq, d)


def kernel(x, norm1_g, w_in, attn_q_norm_g, attn_k_norm_g, attn_sink, gla_wa2_fwd, gla_ba_fwd, gla_wa2_bwd, gla_ba_bwd, gla_out_norm_g, w_out, norm2_g, w_up, conv_w, conv_b, w_down):
    for l in range(w_in.shape[0]):
        x = _layer(x, norm1_g[l], w_in[l], attn_q_norm_g[l], attn_k_norm_g[l], attn_sink[l],
                   gla_wa2_fwd[l], gla_ba_fwd[l], gla_wa2_bwd[l], gla_ba_bwd[l], gla_out_norm_g[l],
                   w_out[l], norm2_g[l], w_up[l], conv_w[l], conv_b[l], w_down[l])
    return x
```

```python
import functools

import jax
import jax.numpy as jnp
import numpy as np
from jax import lax
from jax.experimental import pallas as pl
from jax.experimental.pallas import tpu as pltpu

F32 = jnp.float32
BF16 = jnp.bfloat16

HEAD_DIM = 128
KV_HEADS = 2
GQA_GROUP = 4
ATTN_HEADS = KV_HEADS * GQA_GROUP
ATTN_WIDTH = ATTN_HEADS * HEAD_DIM
WINDOW = 128
ROPE_THETA = 10000.0
GLA_HEADS = 4
GLA_DK = 128
GLA_DV = 256
GLA_KEY_WIDTH = GLA_HEADS * GLA_DK
GLA_WIDTH = GLA_HEADS * GLA_DV
GLA_DECAY_RANK = 16
GLA_GATE_NORMALIZER = 16.0
GLA_CHUNK = 64
GLA_BLOCK = 256
NORM_EPS = 1e-6

OFF_QA = 0
OFF_KA = OFF_QA + ATTN_WIDTH
OFF_VA = OFF_KA + KV_HEADS * HEAD_DIM
OFF_QG = OFF_VA + KV_HEADS * HEAD_DIM
OFF_KG = OFF_QG + GLA_KEY_WIDTH
OFF_VG = OFF_KG + GLA_KEY_WIDTH
OFF_GG = OFF_VG + GLA_WIDTH
OFF_LR = OFF_GG + GLA_WIDTH
MAIN_WIDTH = OFF_LR
LR_PAD = 128

VMEM_LIMIT_BYTES = 56 * 1024 * 1024
INPROJ_TM = 1024
INPROJ_TN = 1536
ATTN_TQ = 128
ROPE_ROWS = 512
OUTPROJ_TM = 512
FFN_TM = 1024
FFN_TF = 512
FFN_XROWS = 128
HALO = 16


def _cparams(*sem):
    return pltpu.CompilerParams(dimension_semantics=sem, vmem_limit_bytes=VMEM_LIMIT_BYTES)


def _inproj_kernel(x_ref, g_ref, w_ref, wlr_ref, wu_ref, wd_ref, o_ref, lr_ref, wu_o_ref, wd_o_ref, h_ref):
    def cast_ffn_weight_blocks():
        wu_o_ref[...] = wu_ref[...].astype(BF16)
        wd_o_ref[...] = wd_ref[...].astype(BF16)

    @pl.when(pl.program_id(1) == 0)
    def _():
        x = x_ref[...]
        ms = jnp.mean(x * x, axis=-1, keepdims=True)
        h = (x * lax.rsqrt(ms + NORM_EPS) * g_ref[...]).astype(BF16)
        h_ref[...] = h
        o_ref[...] = jnp.dot(h, w_ref[...], preferred_element_type=F32).astype(o_ref.dtype)
        lr_ref[...] = jnp.dot(h, wlr_ref[...], preferred_element_type=F32)
        cast_ffn_weight_blocks()

    @pl.when(pl.program_id(1) > 0)
    def _():
        o_ref[...] = jnp.dot(h_ref[...], w_ref[...], preferred_element_type=F32).astype(o_ref.dtype)
        cast_ffn_weight_blocks()


def _cast_block_count(steps, up_cols, down_rows):
    for n in range(steps, 0, -1):
        if up_cols % (n * 128) == 0 and down_rows % (n * HALO) == 0:
            return n
    raise ValueError("FFN weight shapes cannot be split into cast blocks")


def _inproj(x2, g1, w_main, w_lr, w_up, w_down):
    m, d = x2.shape
    tm, tn = INPROJ_TM, INPROJ_TN
    assert m % tm == 0 and MAIN_WIDTH % tn == 0 and w_main.shape == (d, MAIN_WIDTH + 2 * GLA_DECAY_RANK)
    nj = MAIN_WIDTH // tn
    nb = _cast_block_count((m // tm) * nj, w_up.shape[1], w_down.shape[0])
    ub, db = w_up.shape[1] // nb, w_down.shape[0] // nb

    def cast_block(i, j):
        return jnp.minimum(i * nj + j, nb - 1)

    return pl.pallas_call(
        _inproj_kernel,
        grid=(m // tm, nj),
        in_specs=[
            pl.BlockSpec((tm, d), lambda i, j: (i, 0)),
            pl.BlockSpec((1, d), lambda i, j: (0, 0)),
            pl.BlockSpec((d, tn), lambda i, j: (0, j)),
            pl.BlockSpec((d, LR_PAD), lambda i, j: (0, 0)),
            pl.BlockSpec((d, ub), lambda i, j: (0, cast_block(i, j))),
            pl.BlockSpec((db, d), lambda i, j: (cast_block(i, j), 0)),
        ],
        out_specs=[
            pl.BlockSpec((tm, tn), lambda i, j: (i, j)),
            pl.BlockSpec((tm, LR_PAD), lambda i, j: (i, 0)),
            pl.BlockSpec((d, ub), lambda i, j: (0, cast_block(i, j))),
            pl.BlockSpec((db, d), lambda i, j: (cast_block(i, j), 0)),
        ],
        out_shape=[
            jax.ShapeDtypeStruct((m, MAIN_WIDTH), BF16),
            jax.ShapeDtypeStruct((m, LR_PAD), F32),
            jax.ShapeDtypeStruct(w_up.shape, BF16),
            jax.ShapeDtypeStruct(w_down.shape, BF16),
        ],
        scratch_shapes=[pltpu.VMEM((tm, d), BF16)],
        compiler_params=_cparams("arbitrary", "arbitrary"),
        name="inproj",
    )(x2, g1, w_main, w_lr, w_up, w_down)


def _norm_rope(t, gain, cos, sin):
    t = t * lax.rsqrt(jnp.mean(t * t, axis=-1, keepdims=True) + NORM_EPS) * gain
    return t * cos + pltpu.roll(t, HEAD_DIM // 2, 1) * sin


def _attn_kernel(sink_ref, q_ref, k_ref, v_ref, cos_ref, sin_ref, qg_ref, kg_ref, o_ref, kr_ref,
                 *, tq, seq):
    i = pl.program_id(1)

    @pl.when(i == 0)
    def _():
        def body(c, carry):
            r0 = pl.multiple_of(c * ROPE_ROWS, ROPE_ROWS)
            cs = cos_ref[pl.ds(r0, ROPE_ROWS), :]
            sn = sin_ref[pl.ds(r0, ROPE_ROWS), :]
            for h in range(KV_HEADS):
                cols = slice(h * HEAD_DIM, (h + 1) * HEAD_DIM)
                kk = k_ref[0, pl.ds(r0, ROPE_ROWS), cols].astype(F32)
                kr_ref[pl.ds(r0, ROPE_ROWS), cols] = _norm_rope(kk, kg_ref[...], cs, sn).astype(BF16)
            return carry

        lax.fori_loop(0, seq // ROPE_ROWS, body, 0)

    win = tq + 2 * WINDOW
    q0 = pl.multiple_of(i * tq, tq)
    start = pl.multiple_of(jnp.clip(q0 - WINDOW, 0, seq - win), WINDOW)
    cs = cos_ref[pl.ds(q0, tq), :]
    sn = sin_ref[pl.ds(q0, tq), :]
    row = lax.broadcasted_iota(jnp.int32, (tq, win), 0)
    col = lax.broadcasted_iota(jnp.int32, (tq, win), 1)
    mask = (jnp.abs((start + col) - (q0 + row)) <= WINDOW)[None]
    grp = lax.broadcasted_iota(jnp.int32, (GQA_GROUP, 1, 1), 0)
    scale = HEAD_DIM ** -0.5

    for kvh in range(KV_HEADS):
        cols = slice(kvh * HEAD_DIM, (kvh + 1) * HEAD_DIM)
        kw = kr_ref[pl.ds(start, win), cols]
        vw = v_ref[0, pl.ds(start, win), cols]
        qs = []
        sink = jnp.zeros((GQA_GROUP, 1, 1), F32)
        for g in range(GQA_GROUP):
            h = kvh * GQA_GROUP + g
            qq = q_ref[0, :, h * HEAD_DIM:(h + 1) * HEAD_DIM].astype(F32)
            qs.append((_norm_rope(qq, qg_ref[...], cs, sn) * scale).astype(BF16))
            sink = jnp.where(grp == g, sink_ref[h], sink)
        qst = jnp.concatenate(qs, axis=0)
        s = lax.dot_general(qst, kw, (((1,), (1,)), ((), ())), preferred_element_type=F32)
        s = jnp.where(mask, s.reshape(GQA_GROUP, tq, win), -jnp.inf)
        m = jnp.maximum(jnp.max(s, axis=-1, keepdims=True), sink)
        p = jnp.exp(s - m)
        denom = jnp.sum(p, axis=-1, keepdims=True) + jnp.exp(sink - m)
        o = jnp.dot(p.astype(BF16).reshape(GQA_GROUP * tq, win), vw, preferred_element_type=F32)
        o = o.reshape(GQA_GROUP, tq, HEAD_DIM) / denom
        for g in range(GQA_GROUP):
            h = kvh * GQA_GROUP + g
            o_ref[0, :, h * HEAD_DIM:(h + 1) * HEAD_DIM] = o[g].astype(o_ref.dtype)


def _attention(proj3, cos, sin, qg, kg, sink):
    b, seq, _ = proj3.shape
    tq = ATTN_TQ
    kvw = KV_HEADS * HEAD_DIM
    assert seq % tq == 0 and seq % ROPE_ROWS == 0 and seq >= tq + 2 * WINDOW and tq % WINDOW == 0
    return pl.pallas_call(
        functools.partial(_attn_kernel, tq=tq, seq=seq),
        grid=(b, seq // tq),
        in_specs=[
            pl.BlockSpec(memory_space=pltpu.SMEM),
            pl.BlockSpec((1, tq, ATTN_WIDTH), lambda bi, i: (bi, i, OFF_QA // ATTN_WIDTH)),
            pl.BlockSpec((1, seq, kvw), lambda bi, i: (bi, 0, OFF_KA // kvw)),
            pl.BlockSpec((1, seq, kvw), lambda bi, i: (bi, 0, OFF_VA // kvw)),
            pl.BlockSpec((seq, HEAD_DIM), lambda bi, i: (0, 0)),
            pl.BlockSpec((seq, HEAD_DIM), lambda bi, i: (0, 0)),
            pl.BlockSpec((1, HEAD_DIM), lambda bi, i: (0, 0)),
            pl.BlockSpec((1, HEAD_DIM), lambda bi, i: (0, 0)),
        ],
        out_specs=pl.BlockSpec((1, tq, ATTN_WIDTH), lambda bi, i: (bi, i, 0)),
        out_shape=jax.ShapeDtypeStruct((b, seq, ATTN_WIDTH), BF16),
        scratch_shapes=[pltpu.VMEM((seq, kvw), BF16)],
        compiler_params=_cparams("parallel", "arbitrary"),
        name="swa",
    )(sink, proj3, proj3, proj3, cos, sin, qg, kg)


def _log_sigmoid(z):
    return jnp.minimum(z, 0.0) - jnp.log(1.0 + jnp.exp(-jnp.abs(z)))


def _split2(g):
    hi = g.astype(BF16)
    lo = (g - hi.astype(F32)).astype(BF16)
    return hi, lo


def _gla_kernel(q_ref, k_ref, v_ref, gate_ref, lr_ref, wa_ref, ba_ref, gn_ref,
                o_ref, qin_ref, kv_ref, dec_ref, acc_ref, st_ref, *, seq):
    c = GLA_CHUNK
    n = seq // c
    blk = GLA_BLOCK
    cpb = blk // c

    ri = lax.broadcasted_iota(jnp.int32, (blk, blk), 0)
    ci = lax.broadcasted_iota(jnp.int32, (blk, blk), 1)
    same = (ri // c) == (ci // c)
    lower_incl = same & (ci <= ri)
    upper_strict = same & (ci > ri)
    tri_f = jnp.where(lower_incl, 1.0, 0.0).astype(BF16)
    qscale = GLA_DK ** -0.5

    def intra(t, carry):
        r0 = pl.multiple_of(t * blk, blk)
        lr = lr_ref[0, pl.ds(r0, blk), :].astype(BF16)
        q3 = (q_ref[0, pl.ds(r0, blk), :].astype(F32) * qscale).reshape(cpb, c, GLA_DK)
        k3 = k_ref[0, pl.ds(r0, blk), :].astype(F32).reshape(cpb, c, GLA_DK)
        v = v_ref[0, pl.ds(r0, blk), :]

        z = jnp.dot(lr, wa_ref[...], preferred_element_type=F32) + ba_ref[...]
        g = _log_sigmoid(z) / GLA_GATE_NORMALIZER
        g_hi, g_lo = _split2(g)
        pre = jnp.dot(tri_f, jnp.concatenate([g_hi, g_lo], axis=1), preferred_element_type=F32)
        prefix = pre[:, :2 * GLA_DK] + pre[:, 2 * GLA_DK:]
        bcum_f = prefix[:, :GLA_DK].reshape(cpb, c, GLA_DK)
        pref_b = prefix[:, GLA_DK:].reshape(cpb, c, GLA_DK)
        bcum_b = pref_b[:, c - 1:c, :] - pref_b + g[:, GLA_DK:].reshape(cpb, c, GLA_DK)

        def one_direction(bcum, keep, ref_row, last_row, slot):
            b_mid = bcum[:, ref_row:ref_row + 1, :]
            b_end = bcum[:, last_row:last_row + 1, :]
            qe = q3 * jnp.exp(bcum - b_mid)
            ke = k3 * jnp.exp(b_mid - bcum)
            qin_ref[slot, pl.ds(r0, blk), :] = (qe * jnp.exp(b_mid)).astype(BF16).reshape(blk, GLA_DK)
            kout = (ke * jnp.exp(b_end - b_mid)).astype(BF16)
            for j in range(cpb):
                kv_ref[slot, t * cpb + j] = lax.dot_general(
                    v[j * c:(j + 1) * c], kout[j], (((0,), (0,)), ((), ())), preferred_element_type=F32)
            dec = jnp.broadcast_to(jnp.exp(b_end), (cpb, 8, GLA_DK)).reshape(cpb * 8, GLA_DK)
            dec_ref[slot, pl.ds(pl.multiple_of(t * (cpb * 8), cpb * 8), cpb * 8), :] = dec
            a = lax.dot_general(qe.astype(BF16).reshape(blk, GLA_DK), ke.astype(BF16).reshape(blk, GLA_DK),
                                (((1,), (1,)), ((), ())), preferred_element_type=F32)
            return jnp.where(keep, a, 0.0).astype(BF16)

        a_f = one_direction(bcum_f, lower_incl, c // 2, c - 1, 0)
        a_b = one_direction(bcum_b, upper_strict, c - 1 - c // 2, 0, 1)
        o = jnp.dot(jnp.concatenate([a_f, a_b], axis=1), jnp.concatenate([v, v], axis=0),
                    preferred_element_type=F32)
        acc_ref[pl.ds(r0, blk), :] = o
        return carry

    lax.fori_loop(0, seq // blk, intra, 0, unroll=8)

    def step(idx, slot):
        r0 = pl.multiple_of(idx * c, c)
        st = st_ref[slot]
        o = lax.dot_general(qin_ref[slot, pl.ds(r0, c), :], st.astype(BF16), (((1,), (1,)), ((), ())),
                            preferred_element_type=F32)
        dec = dec_ref[slot, pl.ds(pl.multiple_of(idx * 8, 8), 8), :]
        st_ref[slot] = ((st.reshape(GLA_DV // 8, 8, GLA_DK) * dec[None]).reshape(GLA_DV, GLA_DK)
                        + kv_ref[slot, idx])
        return r0, o

    def accumulate(r0, o):
        acc_ref[pl.ds(r0, c), :] += o

    def finalize(r0, o):
        o = o + acc_ref[pl.ds(r0, c), :]
        y = o * lax.rsqrt(jnp.mean(o * o, axis=-1, keepdims=True) + NORM_EPS) * gn_ref[...]
        gate = gate_ref[0, pl.ds(r0, c), :].astype(F32)
        o_ref[0, pl.ds(r0, c), :] = (y * (gate * jax.nn.sigmoid(gate))).astype(o_ref.dtype)

    st_ref[...] = jnp.zeros_like(st_ref)

    def first_half(t, carry):
        accumulate(*step(t, 0))
        accumulate(*step(n - 1 - t, 1))
        return carry

    def second_half(t, carry):
        finalize(*step(t, 0))
        finalize(*step(n - 1 - t, 1))
        return carry

    lax.fori_loop(0, n // 2, first_half, 0, unroll=4)
    lax.fori_loop(n // 2, n, second_half, 0, unroll=4)


def _gla(proj3, lr3, wa, ba, gn):
    b, seq, _ = proj3.shape
    assert (seq // GLA_CHUNK) % 8 == 0 and seq % GLA_BLOCK == 0
    return pl.pallas_call(
        functools.partial(_gla_kernel, seq=seq),
        grid=(b, GLA_HEADS),
        in_specs=[
            pl.BlockSpec((1, seq, GLA_DK), lambda bi, h: (bi, 0, OFF_QG // GLA_DK + h)),
            pl.BlockSpec((1, seq, GLA_DK), lambda bi, h: (bi, 0, OFF_KG // GLA_DK + h)),
            pl.BlockSpec((1, seq, GLA_DV), lambda bi, h: (bi, 0, OFF_VG // GLA_DV + h)),
            pl.BlockSpec((1, seq, GLA_DV), lambda bi, h: (bi, 0, OFF_GG // GLA_DV + h)),
            pl.BlockSpec((1, seq, LR_PAD), lambda bi, h: (bi, 0, 0)),
            pl.BlockSpec((LR_PAD, 2 * GLA_DK), lambda bi, h: (0, h)),
            pl.BlockSpec((1, 2 * GLA_DK), lambda bi, h: (0, h)),
            pl.BlockSpec((1, GLA_DV), lambda bi, h: (0, 0)),
        ],
        out_specs=pl.BlockSpec((1, seq, GLA_DV), lambda bi, h: (bi, 0, h)),
        out_shape=jax.ShapeDtypeStruct((b, seq, GLA_WIDTH), BF16),
        scratch_shapes=[
            pltpu.VMEM((2, seq, GLA_DK), BF16),
            pltpu.VMEM((2, seq // GLA_CHUNK, GLA_DV, GLA_DK), F32),
            pltpu.VMEM((2, seq // GLA_CHUNK * 8, GLA_DK), F32),
            pltpu.VMEM((seq, GLA_DV), F32),
            pltpu.VMEM((2, GLA_DV, GLA_DK), F32),
        ],
        compiler_params=_cparams("parallel", "arbitrary"),
        name="gla",
    )(proj3, proj3, proj3, proj3, lr3, wa, ba, gn)


def _outproj_kernel(oa_ref, og_ref, x_ref, w_ref, g_ref, x1_ref, h2_ref):
    wa = oa_ref.shape[1]
    acc = jnp.dot(oa_ref[...], w_ref[:wa, :], preferred_element_type=F32)
    acc = acc + jnp.dot(og_ref[...], w_ref[wa:, :], preferred_element_type=F32)
    x1 = x_ref[...] + acc
    x1_ref[...] = x1
    ms = jnp.mean(x1 * x1, axis=-1, keepdims=True)
    h2_ref[...] = (x1 * lax.rsqrt(ms + NORM_EPS) * g_ref[...]).astype(h2_ref.dtype)


def _outproj(oa, og, x2, w_out, g2):
    m, d = x2.shape
    tm = OUTPROJ_TM
    wa, wg = oa.shape[1], og.shape[1]
    assert m % tm == 0 and w_out.shape == (wa + wg, d)
    return pl.pallas_call(
        _outproj_kernel,
        grid=(m // tm,),
        in_specs=[
            pl.BlockSpec((tm, wa), lambda i: (i, 0)),
            pl.BlockSpec((tm, wg), lambda i: (i, 0)),
            pl.BlockSpec((tm, d), lambda i: (i, 0)),
            pl.BlockSpec((wa + wg, d), lambda i: (0, 0)),
            pl.BlockSpec((1, d), lambda i: (0, 0)),
        ],
        out_specs=[
            pl.BlockSpec((tm, d), lambda i: (i, 0)),
            pl.BlockSpec((tm, d), lambda i: (i, 0)),
        ],
        out_shape=[
            jax.ShapeDtypeStruct((m, d), F32),
            jax.ShapeDtypeStruct((m, d), BF16),
        ],
        compiler_params=_cparams("parallel"),
        name="outproj",
    )(oa, og, x2, w_out, g2)


def _ffn_kernel(h_ref, hp_ref, hn_ref, x1_ref, wg_ref, wv_ref, cwg_ref, cwv_ref, cbg_ref, cbv_ref,
                wd_ref, o_ref, hext_ref, *, tm, tiles_per_seq):
    i = pl.program_id(0)
    f = pl.program_id(1)

    th = tm // 2
    rows = th + HALO

    @pl.when(f == 0)
    def _():
        pos = i % tiles_per_seq
        r = lax.broadcasted_iota(jnp.int32, (HALO, 1), 0)
        zero = jnp.zeros_like(hn_ref[...])
        prev0 = jnp.where((r == HALO - 1) & (pos != 0), hp_ref[...], zero)
        next1 = jnp.where((r == 0) & (pos != tiles_per_seq - 1), hn_ref[...], zero)
        hext_ref[0, :th, :] = h_ref[:th, :]
        hext_ref[0, th:, :] = jnp.where(r == 0, h_ref[th:th + HALO, :], prev0)
        hext_ref[1, :th, :] = h_ref[th:, :]
        hext_ref[1, th:, :] = jnp.where(r == HALO - 1, h_ref[th - HALO:th, :], next1)
        o_ref[...] = jnp.zeros_like(o_ref)

    def conv(u, cw_ref, cb_ref):
        prev = pltpu.roll(u, 1, 0)
        nxt = pltpu.roll(u, rows - 1, 0)
        y = prev * cw_ref[0:1, :] + u * cw_ref[1:2, :] + nxt * cw_ref[2:3, :] + cb_ref[...]
        return y[:th]

    acts = []
    for half in range(2):
        hext = hext_ref[half]
        ug = conv(jnp.dot(hext, wg_ref[...], preferred_element_type=F32), cwg_ref, cbg_ref)
        uv = conv(jnp.dot(hext, wv_ref[...], preferred_element_type=F32), cwv_ref, cbv_ref)
        acts.append((ug * jax.nn.sigmoid(ug) * uv).astype(BF16))
    for half in range(2):
        o_ref[half * th:(half + 1) * th, :] += jnp.dot(acts[half], wd_ref[...], preferred_element_type=F32)

    @pl.when(f < tm // FFN_XROWS)
    def _():
        r0 = pl.multiple_of(f * FFN_XROWS, FFN_XROWS)
        o_ref[pl.ds(r0, FFN_XROWS), :] += x1_ref[...]


def _ffn(h2, x1, w_up, conv_w, conv_b, w_down, seq):
    m, d = h2.shape
    dff = w_down.shape[0]
    tm, tf = FFN_TM, FFN_TF
    nf = dff // tf
    hb = tm // HALO
    last_halo = m // HALO - 1
    xch = tm // FFN_XROWS
    assert nf >= xch and dff % tf == 0 and seq % tm == 0 and m % seq == 0 and w_up.shape == (d, 2 * dff)
    return pl.pallas_call(
        functools.partial(_ffn_kernel, tm=tm, tiles_per_seq=seq // tm),
        grid=(m // tm, nf),
        in_specs=[
            pl.BlockSpec((tm, d), lambda i, f: (i, 0)),
            pl.BlockSpec((HALO, d), lambda i, f: (jnp.maximum(i * hb - 1, 0), 0)),
            pl.BlockSpec((HALO, d), lambda i, f: (jnp.minimum((i + 1) * hb, last_halo), 0)),
            pl.BlockSpec((FFN_XROWS, d), lambda i, f: (i * xch + jnp.minimum(f, xch - 1), 0)),
            pl.BlockSpec((d, tf), lambda i, f: (0, f)),
            pl.BlockSpec((d, tf), lambda i, f: (0, nf + f)),
            pl.BlockSpec((3, tf), lambda i, f: (0, f)),
            pl.BlockSpec((3, tf), lambda i, f: (0, nf + f)),
            pl.BlockSpec((1, tf), lambda i, f: (0, f)),
            pl.BlockSpec((1, tf), lambda i, f: (0, nf + f)),
            pl.BlockSpec((tf, d), lambda i, f: (f, 0)),
        ],
        out_specs=pl.BlockSpec((tm, d), lambda i, f: (i, 0)),
        out_shape=jax.ShapeDtypeStruct((m, d), F32),
        scratch_shapes=[
            pltpu.VMEM((2, tm // 2 + HALO, d), BF16),
        ],
        compiler_params=_cparams("parallel", "arbitrary"),
        name="convffn",
    )(h2, h2, h2, x1, w_up, w_up, conv_w, conv_w, conv_b, conv_b, w_down)


def _rope_tables(seq):
    half = HEAD_DIM // 2
    inv = 1.0 / (ROPE_THETA ** (np.arange(half, dtype=np.float64) / half))
    ang = np.arange(seq, dtype=np.float64)[:, None] * inv[None, :]
    cos = np.cos(ang).astype(np.float32)
    sin = np.sin(ang).astype(np.float32)
    return (jnp.asarray(np.concatenate([cos, cos], axis=-1)),
            jnp.asarray(np.concatenate([-sin, sin], axis=-1)))


def _decay_params(wa2_f, ba_f, wa2_b, ba_b):
    r = GLA_DECAY_RANK
    wf = jnp.pad(wa2_f, ((0, LR_PAD - r), (0, 0))).reshape(LR_PAD, GLA_HEADS, 1, GLA_DK)
    wb = jnp.pad(wa2_b, ((r, LR_PAD - 2 * r), (0, 0))).reshape(LR_PAD, GLA_HEADS, 1, GLA_DK)
    wa = jnp.concatenate([wf, wb], axis=2).reshape(LR_PAD, 2 * GLA_KEY_WIDTH).astype(BF16)
    ba = jnp.concatenate([ba_f.reshape(GLA_HEADS, 1, GLA_DK), ba_b.reshape(GLA_HEADS, 1, GLA_DK)],
                         axis=1).reshape(1, 2 * GLA_KEY_WIDTH)
    return wa, ba


def _layer(x, norm1_g, w_in, q_norm_g, k_norm_g, sink, wa2_f, ba_f, wa2_b, ba_b, gla_norm_g,
           w_out, norm2_g, w_up, conv_w, conv_b, w_down):
    b, seq, d = x.shape
    m = b * seq
    x2 = x.reshape(m, d)
    w_lr = jnp.pad(w_in[:, MAIN_WIDTH:], ((0, 0), (0, LR_PAD - 2 * GLA_DECAY_RANK))).astype(BF16)
    proj, lr, w_up_b, w_down_b = _inproj(x2, norm1_g.reshape(1, d), w_in.astype(BF16), w_lr, w_up, w_down)
    proj3 = proj.reshape(b, seq, MAIN_WIDTH)
    cos, sin = _rope_tables(seq)
    o_attn = _attention(proj3, cos, sin, q_norm_g.reshape(1, HEAD_DIM), k_norm_g.reshape(1, HEAD_DIM), sink)
    wa, ba = _decay_params(wa2_f, ba_f, wa2_b, ba_b)
    o_gla = _gla(proj3, lr.reshape(b, seq, LR_PAD), wa, ba, gla_norm_g.reshape(1, GLA_DV))
    x1, h2 = _outproj(o_attn.reshape(m, ATTN_WIDTH), o_gla.reshape(m, GLA_WIDTH), x2,
                      w_out.astype(BF16), norm2_g.reshape(1, d))
    out = _ffn(h2, x1, w_up_b, conv_w, conv_b.reshape(1, -1), w_down_b, seq)
    return out.reshape(b, seq, d)


def kernel(x, norm1_g, w_in, attn_q_norm_g, attn_k_norm_g, attn_sink, gla_wa2_fwd, gla_ba_fwd, gla_wa2_bwd, gla_ba_bwd, gla_out_norm_g, w_out, norm2_g, w_up, conv_w, conv_b, w_down):
    for l in range(w_in.shape[0]):
        x = _layer(x, norm1_g[l], w_in[l], attn_q_norm_g[l], attn_k_norm_g[l], attn_sink[l],
                   gla_wa2_fwd[l], gla_ba_fwd[l], gla_wa2_bwd[l], gla_ba_bwd[l], gla_out_norm_g[l],
                   w_out[l], norm2_g[l], w_up[l], conv_w[l], conv_b[l], w_down[l])
    return x
```

```python
import functools

import jax
import jax.numpy as jnp
import numpy as np
from jax import lax
from jax.experimental import pallas as pl
from jax.experimental.pallas import tpu as pltpu

F32 = jnp.float32
BF16 = jnp.bfloat16

HEAD_DIM = 128
KV_HEADS = 2
GQA_GROUP = 4
ATTN_HEADS = KV_HEADS * GQA_GROUP
ATTN_WIDTH = ATTN_HEADS * HEAD_DIM
WINDOW = 128
ROPE_THETA = 10000.0
GLA_HEADS = 4
GLA_DK = 128
GLA_DV = 256
GLA_KEY_WIDTH = GLA_HEADS * GLA_DK
GLA_WIDTH = GLA_HEADS * GLA_DV
GLA_DECAY_RANK = 16
GLA_GATE_NORMALIZER = 16.0
GLA_CHUNK = 64
GLA_BLOCK = 256
NORM_EPS = 1e-6

OFF_QA = 0
OFF_KA = OFF_QA + ATTN_WIDTH
OFF_VA = OFF_KA + KV_HEADS * HEAD_DIM
OFF_QG = OFF_VA + KV_HEADS * HEAD_DIM
OFF_KG = OFF_QG + GLA_KEY_WIDTH
OFF_VG = OFF_KG + GLA_KEY_WIDTH
OFF_GG = OFF_VG + GLA_WIDTH
OFF_LR = OFF_GG + GLA_WIDTH
MAIN_WIDTH = OFF_LR
LR_PAD = 128

VMEM_LIMIT_BYTES = 56 * 1024 * 1024
INPROJ_TM = 1024
INPROJ_TN = 1536
ATTN_TQ = 128
ROPE_ROWS = 512
OUTPROJ_TM = 512
FFN_TM = 1024
FFN_TF = 512
FFN_XROWS = 128
HALO = 16


def _cparams(*sem):
    return pltpu.CompilerParams(dimension_semantics=sem, vmem_limit_bytes=VMEM_LIMIT_BYTES)


def _inproj_kernel(x_ref, g_ref, w_ref, wlr_ref, wu_ref, wd_ref, wo_ref,
                   o_ref, lr_ref, wu_o_ref, wd_o_ref, wo_o_ref, h_ref):
    def cast_weight_blocks():
        wu_o_ref[...] = wu_ref[...].astype(BF16)
        wd_o_ref[...] = wd_ref[...].astype(BF16)
        wo_o_ref[...] = wo_ref[...].astype(BF16)

    @pl.when(pl.program_id(1) == 0)
    def _():
        x = x_ref[...]
        ms = jnp.mean(x * x, axis=-1, keepdims=True)
        h = (x * lax.rsqrt(ms + NORM_EPS) * g_ref[...]).astype(BF16)
        h_ref[...] = h
        o_ref[...] = jnp.dot(h, w_ref[...], preferred_element_type=F32).astype(o_ref.dtype)
        lr_ref[...] = jnp.dot(h, wlr_ref[...], preferred_element_type=F32)
        cast_weight_blocks()

    @pl.when(pl.program_id(1) > 0)
    def _():
        o_ref[...] = jnp.dot(h_ref[...], w_ref[...], preferred_element_type=F32).astype(o_ref.dtype)
        cast_weight_blocks()


def _cast_block_count(steps, extent, unit):
    return max(n for n in range(1, steps + 1) if extent % (n * unit) == 0)


def _inproj(x2, g1, w_main, w_lr, w_up, w_down, w_out):
    m, d = x2.shape
    tm, tn = INPROJ_TM, INPROJ_TN
    assert m % tm == 0 and MAIN_WIDTH % tn == 0 and w_main.shape == (d, MAIN_WIDTH + 2 * GLA_DECAY_RANK)
    nj = MAIN_WIDTH // tn
    steps = (m // tm) * nj
    nu = _cast_block_count(steps, w_up.shape[1], 128)
    nd = _cast_block_count(steps, w_down.shape[0], HALO)
    no = _cast_block_count(steps, w_out.shape[0], HALO)
    ub, db, ob = w_up.shape[1] // nu, w_down.shape[0] // nd, w_out.shape[0] // no

    def cast_block(n):
        return lambda i, j: jnp.minimum(i * nj + j, n - 1)

    return pl.pallas_call(
        _inproj_kernel,
        grid=(m // tm, nj),
        in_specs=[
            pl.BlockSpec((tm, d), lambda i, j: (i, 0)),
            pl.BlockSpec((1, d), lambda i, j: (0, 0)),
            pl.BlockSpec((d, tn), lambda i, j: (0, j)),
            pl.BlockSpec((d, LR_PAD), lambda i, j: (0, 0)),
            pl.BlockSpec((d, ub), lambda i, j: (0, cast_block(nu)(i, j))),
            pl.BlockSpec((db, d), lambda i, j: (cast_block(nd)(i, j), 0)),
            pl.BlockSpec((ob, d), lambda i, j: (cast_block(no)(i, j), 0)),
        ],
        out_specs=[
            pl.BlockSpec((tm, tn), lambda i, j: (i, j)),
            pl.BlockSpec((tm, LR_PAD), lambda i, j: (i, 0)),
            pl.BlockSpec((d, ub), lambda i, j: (0, cast_block(nu)(i, j))),
            pl.BlockSpec((db, d), lambda i, j: (cast_block(nd)(i, j), 0)),
            pl.BlockSpec((ob, d), lambda i, j: (cast_block(no)(i, j), 0)),
        ],
        out_shape=[
            jax.ShapeDtypeStruct((m, MAIN_WIDTH), BF16),
            jax.ShapeDtypeStruct((m, LR_PAD), F32),
            jax.ShapeDtypeStruct(w_up.shape, BF16),
            jax.ShapeDtypeStruct(w_down.shape, BF16),
            jax.ShapeDtypeStruct(w_out.shape, BF16),
        ],
        scratch_shapes=[pltpu.VMEM((tm, d), BF16)],
        compiler_params=_cparams("arbitrary", "arbitrary"),
        name="inproj",
    )(x2, g1, w_main, w_lr, w_up, w_down, w_out)


def _norm_rope(t, gain, cos, sin):
    t = t * lax.rsqrt(jnp.mean(t * t, axis=-1, keepdims=True) + NORM_EPS) * gain
    return t * cos + pltpu.roll(t, HEAD_DIM // 2, 1) * sin


def _attn_kernel(sink_ref, q_ref, k_ref, v_ref, cos_ref, sin_ref, qg_ref, kg_ref, o_ref, kr_ref,
                 *, tq, seq):
    i = pl.program_id(1)

    @pl.when(i == 0)
    def _():
        def body(c, carry):
            r0 = pl.multiple_of(c * ROPE_ROWS, ROPE_ROWS)
            cs = cos_ref[pl.ds(r0, ROPE_ROWS), :]
            sn = sin_ref[pl.ds(r0, ROPE_ROWS), :]
            for h in range(KV_HEADS):
                cols = slice(h * HEAD_DIM, (h + 1) * HEAD_DIM)
                kk = k_ref[0, pl.ds(r0, ROPE_ROWS), cols].astype(F32)
                kr_ref[pl.ds(r0, ROPE_ROWS), cols] = _norm_rope(kk, kg_ref[...], cs, sn).astype(BF16)
            return carry

        lax.fori_loop(0, seq // ROPE_ROWS, body, 0)

    win = tq + 2 * WINDOW
    q0 = pl.multiple_of(i * tq, tq)
    start = pl.multiple_of(jnp.clip(q0 - WINDOW, 0, seq - win), WINDOW)
    cs = cos_ref[pl.ds(q0, tq), :]
    sn = sin_ref[pl.ds(q0, tq), :]
    row = lax.broadcasted_iota(jnp.int32, (tq, win), 0)
    col = lax.broadcasted_iota(jnp.int32, (tq, win), 1)
    mask = (jnp.abs((start + col) - (q0 + row)) <= WINDOW)[None]
    grp = lax.broadcasted_iota(jnp.int32, (GQA_GROUP, 1, 1), 0)
    scale = HEAD_DIM ** -0.5

    for kvh in range(KV_HEADS):
        cols = slice(kvh * HEAD_DIM, (kvh + 1) * HEAD_DIM)
        kw = kr_ref[pl.ds(start, win), cols]
        vw = v_ref[0, pl.ds(start, win), cols]
        qs = []
        sink = jnp.zeros((GQA_GROUP, 1, 1), F32)
        for g in range(GQA_GROUP):
            h = kvh * GQA_GROUP + g
            qq = q_ref[0, :, h * HEAD_DIM:(h + 1) * HEAD_DIM].astype(F32)
            qs.append((_norm_rope(qq, qg_ref[...], cs, sn) * scale).astype(BF16))
            sink = jnp.where(grp == g, sink_ref[h], sink)
        qst = jnp.concatenate(qs, axis=0)
        s = lax.dot_general(qst, kw, (((1,), (1,)), ((), ())), preferred_element_type=F32)
        s = jnp.where(mask, s.reshape(GQA_GROUP, tq, win), -jnp.inf)
        m = jnp.maximum(jnp.max(s, axis=-1, keepdims=True), sink)
        p = jnp.exp(s - m)
        denom = jnp.sum(p, axis=-1, keepdims=True) + jnp.exp(sink - m)
        o = jnp.dot(p.astype(BF16).reshape(GQA_GROUP * tq, win), vw, preferred_element_type=F32)
        o = o.reshape(GQA_GROUP, tq, HEAD_DIM) / denom
        for g in range(GQA_GROUP):
            h = kvh * GQA_GROUP + g
            o_ref[0, :, h * HEAD_DIM:(h + 1) * HEAD_DIM] = o[g].astype(o_ref.dtype)


def _attention(proj3, cos, sin, qg, kg, sink):
    b, seq, _ = proj3.shape
    tq = ATTN_TQ
    kvw = KV_HEADS * HEAD_DIM
    assert seq % tq == 0 and seq % ROPE_ROWS == 0 and seq >= tq + 2 * WINDOW and tq % WINDOW == 0
    return pl.pallas_call(
        functools.partial(_attn_kernel, tq=tq, seq=seq),
        grid=(b, seq // tq),
        in_specs=[
            pl.BlockSpec(memory_space=pltpu.SMEM),
            pl.BlockSpec((1, tq, ATTN_WIDTH), lambda bi, i: (bi, i, OFF_QA // ATTN_WIDTH)),
            pl.BlockSpec((1, seq, kvw), lambda bi, i: (bi, 0, OFF_KA // kvw)),
            pl.BlockSpec((1, seq, kvw), lambda bi, i: (bi, 0, OFF_VA // kvw)),
            pl.BlockSpec((seq, HEAD_DIM), lambda bi, i: (0, 0)),
            pl.BlockSpec((seq, HEAD_DIM), lambda bi, i: (0, 0)),
            pl.BlockSpec((1, HEAD_DIM), lambda bi, i: (0, 0)),
            pl.BlockSpec((1, HEAD_DIM), lambda bi, i: (0, 0)),
        ],
        out_specs=pl.BlockSpec((1, tq, ATTN_WIDTH), lambda bi, i: (bi, i, 0)),
        out_shape=jax.ShapeDtypeStruct((b, seq, ATTN_WIDTH), BF16),
        scratch_shapes=[pltpu.VMEM((seq, kvw), BF16)],
        compiler_params=_cparams("parallel", "arbitrary"),
        name="swa",
    )(sink, proj3, proj3, proj3, cos, sin, qg, kg)


def _log_sigmoid(z):
    return jnp.minimum(z, 0.0) - jnp.log(1.0 + jnp.exp(-jnp.abs(z)))


def _split2(g):
    hi = g.astype(BF16)
    lo = (g - hi.astype(F32)).astype(BF16)
    return hi, lo


def _gla_kernel(q_ref, k_ref, v_ref, gate_ref, lr_ref, wa_ref, ba_ref, gn_ref,
                o_ref, qin_ref, kv_ref, dec_ref, acc_ref, st_ref, *, seq):
    c = GLA_CHUNK
    n = seq // c
    blk = GLA_BLOCK
    cpb = blk // c

    ri = lax.broadcasted_iota(jnp.int32, (blk, blk), 0)
    ci = lax.broadcasted_iota(jnp.int32, (blk, blk), 1)
    same = (ri // c) == (ci // c)
    lower_incl = same & (ci <= ri)
    upper_strict = same & (ci > ri)
    tri_f = jnp.where(lower_incl, 1.0, 0.0).astype(BF16)
    qscale = GLA_DK ** -0.5

    def intra(t, carry):
        r0 = pl.multiple_of(t * blk, blk)
        lr = lr_ref[0, pl.ds(r0, blk), :].astype(BF16)
        q3 = (q_ref[0, pl.ds(r0, blk), :].astype(F32) * qscale).reshape(cpb, c, GLA_DK)
        k3 = k_ref[0, pl.ds(r0, blk), :].astype(F32).reshape(cpb, c, GLA_DK)
        v = v_ref[0, pl.ds(r0, blk), :]

        z = jnp.dot(lr, wa_ref[...], preferred_element_type=F32) + ba_ref[...]
        g = _log_sigmoid(z) / GLA_GATE_NORMALIZER
        g_hi, g_lo = _split2(g)
        pre = jnp.dot(tri_f, jnp.concatenate([g_hi, g_lo], axis=1), preferred_element_type=F32)
        prefix = pre[:, :2 * GLA_DK] + pre[:, 2 * GLA_DK:]
        bcum_f = prefix[:, :GLA_DK].reshape(cpb, c, GLA_DK)
        pref_b = prefix[:, GLA_DK:].reshape(cpb, c, GLA_DK)
        bcum_b = pref_b[:, c - 1:c, :] - pref_b + g[:, GLA_DK:].reshape(cpb, c, GLA_DK)

        def one_direction(bcum, keep, ref_row, last_row, slot):
            b_mid = bcum[:, ref_row:ref_row + 1, :]
            b_end = bcum[:, last_row:last_row + 1, :]
            qe = q3 * jnp.exp(bcum - b_mid)
            ke = k3 * jnp.exp(b_mid - bcum)
            qin_ref[slot, pl.ds(r0, blk), :] = (qe * jnp.exp(b_mid)).astype(BF16).reshape(blk, GLA_DK)
            kout = (ke * jnp.exp(b_end - b_mid)).astype(BF16)
            for j in range(cpb):
                kv_ref[slot, t * cpb + j] = lax.dot_general(
                    v[j * c:(j + 1) * c], kout[j], (((0,), (0,)), ((), ())), preferred_element_type=F32)
            dec = jnp.broadcast_to(jnp.exp(b_end), (cpb, 8, GLA_DK)).reshape(cpb * 8, GLA_DK)
            dec_ref[slot, pl.ds(pl.multiple_of(t * (cpb * 8), cpb * 8), cpb * 8), :] = dec
            a = lax.dot_general(qe.astype(BF16).reshape(blk, GLA_DK), ke.astype(BF16).reshape(blk, GLA_DK),
                                (((1,), (1,)), ((), ())), preferred_element_type=F32)
            return jnp.where(keep, a, 0.0).astype(BF16)

        a_f = one_direction(bcum_f, lower_incl, c // 2, c - 1, 0)
        a_b = one_direction(bcum_b, upper_strict, c - 1 - c // 2, 0, 1)
        o = jnp.dot(jnp.concatenate([a_f, a_b], axis=1), jnp.concatenate([v, v], axis=0),
                    preferred_element_type=F32)
        acc_ref[pl.ds(r0, blk), :] = o
        return carry

    lax.fori_loop(0, seq // blk, intra, 0, unroll=8)

    def step(idx, slot):
        r0 = pl.multiple_of(idx * c, c)
        st = st_ref[slot]
        o = lax.dot_general(qin_ref[slot, pl.ds(r0, c), :], st.astype(BF16), (((1,), (1,)), ((), ())),
                            preferred_element_type=F32)
        dec = dec_ref[slot, pl.ds(pl.multiple_of(idx * 8, 8), 8), :]
        st_ref[slot] = ((st.reshape(GLA_DV // 8, 8, GLA_DK) * dec[None]).reshape(GLA_DV, GLA_DK)
                        + kv_ref[slot, idx])
        return r0, o

    def accumulate(r0, o):
        acc_ref[pl.ds(r0, c), :] += o

    def finalize(r0, o):
        o = o + acc_ref[pl.ds(r0, c), :]
        y = o * lax.rsqrt(jnp.mean(o * o, axis=-1, keepdims=True) + NORM_EPS) * gn_ref[...]
        gate = gate_ref[0, pl.ds(r0, c), :].astype(F32)
        o_ref[0, pl.ds(r0, c), :] = (y * (gate * jax.nn.sigmoid(gate))).astype(o_ref.dtype)

    st_ref[...] = jnp.zeros_like(st_ref)

    def first_half(t, carry):
        accumulate(*step(t, 0))
        accumulate(*step(n - 1 - t, 1))
        return carry

    def second_half(t, carry):
        finalize(*step(t, 0))
        finalize(*step(n - 1 - t, 1))
        return carry

    lax.fori_loop(0, n // 2, first_half, 0, unroll=4)
    lax.fori_loop(n // 2, n, second_half, 0, unroll=4)


def _gla(proj3, lr3, wa, ba, gn):
    b, seq, _ = proj3.shape
    assert (seq // GLA_CHUNK) % 8 == 0 and seq % GLA_BLOCK == 0
    return pl.pallas_call(
        functools.partial(_gla_kernel, seq=seq),
        grid=(b, GLA_HEADS),
        in_specs=[
            pl.BlockSpec((1, seq, GLA_DK), lambda bi, h: (bi, 0, OFF_QG // GLA_DK + h)),
            pl.BlockSpec((1, seq, GLA_DK), lambda bi, h: (bi, 0, OFF_KG // GLA_DK + h)),
            pl.BlockSpec((1, seq, GLA_DV), lambda bi, h: (bi, 0, OFF_VG // GLA_DV + h)),
            pl.BlockSpec((1, seq, GLA_DV), lambda bi, h: (bi, 0, OFF_GG // GLA_DV + h)),
            pl.BlockSpec((1, seq, LR_PAD), lambda bi, h: (bi, 0, 0)),
            pl.BlockSpec((LR_PAD, 2 * GLA_DK), lambda bi, h: (0, h)),
            pl.BlockSpec((1, 2 * GLA_DK), lambda bi, h: (0, h)),
            pl.BlockSpec((1, GLA_DV), lambda bi, h: (0, 0)),
        ],
        out_specs=pl.BlockSpec((1, seq, GLA_DV), lambda bi, h: (bi, 0, h)),
        out_shape=jax.ShapeDtypeStruct((b, seq, GLA_WIDTH), BF16),
        scratch_shapes=[
            pltpu.VMEM((2, seq, GLA_DK), BF16),
            pltpu.VMEM((2, seq // GLA_CHUNK, GLA_DV, GLA_DK), F32),
            pltpu.VMEM((2, seq // GLA_CHUNK * 8, GLA_DK), F32),
            pltpu.VMEM((seq, GLA_DV), F32),
            pltpu.VMEM((2, GLA_DV, GLA_DK), F32),
        ],
        compiler_params=_cparams("parallel", "arbitrary"),
        name="gla",
    )(proj3, proj3, proj3, proj3, lr3, wa, ba, gn)


def _outproj_kernel(oa_ref, og_ref, x_ref, w_ref, g_ref, x1_ref, h2_ref):
    wa = oa_ref.shape[1]
    acc = jnp.dot(oa_ref[...], w_ref[:wa, :], preferred_element_type=F32)
    acc = acc + jnp.dot(og_ref[...], w_ref[wa:, :], preferred_element_type=F32)
    x1 = x_ref[...] + acc
    x1_ref[...] = x1
    ms = jnp.mean(x1 * x1, axis=-1, keepdims=True)
    h2_ref[...] = (x1 * lax.rsqrt(ms + NORM_EPS) * g_ref[...]).astype(h2_ref.dtype)


def _outproj(oa, og, x2, w_out, g2):
    m, d = x2.shape
    tm = OUTPROJ_TM
    wa, wg = oa.shape[1], og.shape[1]
    assert m % tm == 0 and w_out.shape == (wa + wg, d)
    return pl.pallas_call(
        _outproj_kernel,
        grid=(m // tm,),
        in_specs=[
            pl.BlockSpec((tm, wa), lambda i: (i, 0)),
            pl.BlockSpec((tm, wg), lambda i: (i, 0)),
            pl.BlockSpec((tm, d), lambda i: (i, 0)),
            pl.BlockSpec((wa + wg, d), lambda i: (0, 0)),
            pl.BlockSpec((1, d), lambda i: (0, 0)),
        ],
        out_specs=[
            pl.BlockSpec((tm, d), lambda i: (i, 0)),
            pl.BlockSpec((tm, d), lambda i: (i, 0)),
        ],
        out_shape=[
            jax.ShapeDtypeStruct((m, d), F32),
            jax.ShapeDtypeStruct((m, d), BF16),
        ],
        compiler_params=_cparams("parallel"),
        name="outproj",
    )(oa, og, x2, w_out, g2)


def _ffn_kernel(h_ref, hp_ref, hn_ref, x1_ref, wg_ref, wv_ref, cwg_ref, cwv_ref, cbg_ref, cbv_ref,
                wd_ref, o_ref, hext_ref, *, tm, tiles_per_seq):
    i = pl.program_id(0)
    f = pl.program_id(1)

    th = tm // 2
    rows = th + HALO

    @pl.when(f == 0)
    def _():
        pos = i % tiles_per_seq
        r = lax.broadcasted_iota(jnp.int32, (HALO, 1), 0)
        zero = jnp.zeros_like(hn_ref[...])
        prev0 = jnp.where((r == HALO - 1) & (pos != 0), hp_ref[...], zero)
        next1 = jnp.where((r == 0) & (pos != tiles_per_seq - 1), hn_ref[...], zero)
        hext_ref[0, :th, :] = h_ref[:th, :]
        hext_ref[0, th:, :] = jnp.where(r == 0, h_ref[th:th + HALO, :], prev0)
        hext_ref[1, :th, :] = h_ref[th:, :]
        hext_ref[1, th:, :] = jnp.where(r == HALO - 1, h_ref[th - HALO:th, :], next1)
        o_ref[...] = jnp.zeros_like(o_ref)

    def conv(u, cw_ref, cb_ref):
        prev = pltpu.roll(u, 1, 0)
        nxt = pltpu.roll(u, rows - 1, 0)
        y = prev * cw_ref[0:1, :] + u * cw_ref[1:2, :] + nxt * cw_ref[2:3, :] + cb_ref[...]
        return y[:th]

    acts = []
    for half in range(2):
        hext = hext_ref[half]
        ug = conv(jnp.dot(hext, wg_ref[...], preferred_element_type=F32), cwg_ref, cbg_ref)
        uv = conv(jnp.dot(hext, wv_ref[...], preferred_element_type=F32), cwv_ref, cbv_ref)
        acts.append((ug * jax.nn.sigmoid(ug) * uv).astype(BF16))
    for half in range(2):
        o_ref[half * th:(half + 1) * th, :] += jnp.dot(acts[half], wd_ref[...], preferred_element_type=F32)

    @pl.when(f < tm // FFN_XROWS)
    def _():
        r0 = pl.multiple_of(f * FFN_XROWS, FFN_XROWS)
        o_ref[pl.ds(r0, FFN_XROWS), :] += x1_ref[...]


def _ffn(h2, x1, w_up, conv_w, conv_b, w_down, seq):
    m, d = h2.shape
    dff = w_down.shape[0]
    tm, tf = FFN_TM, FFN_TF
    nf = dff // tf
    hb = tm // HALO
    last_halo = m // HALO - 1
    xch = tm // FFN_XROWS
    assert nf >= xch and dff % tf == 0 and seq % tm == 0 and m % seq == 0 and w_up.shape == (d, 2 * dff)
    return pl.pallas_call(
        functools.partial(_ffn_kernel, tm=tm, tiles_per_seq=seq // tm),
        grid=(m // tm, nf),
        in_specs=[
            pl.BlockSpec((tm, d), lambda i, f: (i, 0)),
            pl.BlockSpec((HALO, d), lambda i, f: (jnp.maximum(i * hb - 1, 0), 0)),
            pl.BlockSpec((HALO, d), lambda i, f: (jnp.minimum((i + 1) * hb, last_halo), 0)),
            pl.BlockSpec((FFN_XROWS, d), lambda i, f: (i * xch + jnp.minimum(f, xch - 1), 0)),
            pl.BlockSpec((d, tf), lambda i, f: (0, f)),
            pl.BlockSpec((d, tf), lambda i, f: (0, nf + f)),
            pl.BlockSpec((3, tf), lambda i, f: (0, f)),
            pl.BlockSpec((3, tf), lambda i, f: (0, nf + f)),
            pl.BlockSpec((1, tf), lambda i, f: (0, f)),
            pl.BlockSpec((1, tf), lambda i, f: (0, nf + f)),
            pl.BlockSpec((tf, d), lambda i, f: (f, 0)),
        ],
        out_specs=pl.BlockSpec((tm, d), lambda i, f: (i, 0)),
        out_shape=jax.ShapeDtypeStruct((m, d), F32),
        scratch_shapes=[
            pltpu.VMEM((2, tm // 2 + HALO, d), BF16),
        ],
        compiler_params=_cparams("parallel", "arbitrary"),
        name="convffn",
    )(h2, h2, h2, x1, w_up, w_up, conv_w, conv_w, conv_b, conv_b, w_down)


def _rope_tables(seq):
    half = HEAD_DIM // 2
    inv = 1.0 / (ROPE_THETA ** (np.arange(half, dtype=np.float64) / half))
    ang = np.arange(seq, dtype=np.float64)[:, None] * inv[None, :]
    cos = np.cos(ang).astype(np.float32)
    sin = np.sin(ang).astype(np.float32)
    return (jnp.asarray(np.concatenate([cos, cos], axis=-1)),
            jnp.asarray(np.concatenate([-sin, sin], axis=-1)))


def _decay_params(wa2_f, ba_f, wa2_b, ba_b):
    r = GLA_DECAY_RANK
    wf = jnp.pad(wa2_f, ((0, LR_PAD - r), (0, 0))).reshape(LR_PAD, GLA_HEADS, 1, GLA_DK)
    wb = jnp.pad(wa2_b, ((r, LR_PAD - 2 * r), (0, 0))).reshape(LR_PAD, GLA_HEADS, 1, GLA_DK)
    wa = jnp.concatenate([wf, wb], axis=2).reshape(LR_PAD, 2 * GLA_KEY_WIDTH).astype(BF16)
    ba = jnp.concatenate([ba_f.reshape(GLA_HEADS, 1, GLA_DK), ba_b.reshape(GLA_HEADS, 1, GLA_DK)],
                         axis=1).reshape(1, 2 * GLA_KEY_WIDTH)
    return wa, ba


def _layer(x, norm1_g, w_in, q_norm_g, k_norm_g, sink, wa2_f, ba_f, wa2_b, ba_b, gla_norm_g,
           w_out, norm2_g, w_up, conv_w, conv_b, w_down):
    b, seq, d = x.shape
    m = b * seq
    x2 = x.reshape(m, d)
    w_lr = jnp.pad(w_in[:, MAIN_WIDTH:], ((0, 0), (0, LR_PAD - 2 * GLA_DECAY_RANK))).astype(BF16)
    proj, lr, w_up_b, w_down_b, w_out_b = _inproj(x2, norm1_g.reshape(1, d), w_in.astype(BF16), w_lr,
                                                  w_up, w_down, w_out)
    proj3 = proj.reshape(b, seq, MAIN_WIDTH)
    cos, sin = _rope_tables(seq)
    o_attn = _attention(proj3, cos, sin, q_norm_g.reshape(1, HEAD_DIM), k_norm_g.reshape(1, HEAD_DIM), sink)
    wa, ba = _decay_params(wa2_f, ba_f, wa2_b, ba_b)
    o_gla = _gla(proj3, lr.reshape(b, seq, LR_PAD), wa, ba, gla_norm_g.reshape(1, GLA_DV))
    x1, h2 = _outproj(o_attn.reshape(m, ATTN_WIDTH), o_gla.reshape(m, GLA_WIDTH), x2,
                      w_out_b, norm2_g.reshape(1, d))
    out = _ffn(h2, x1, w_up_b, conv_w, conv_b.reshape(1, -1), w_down_b, seq)
    return out.reshape(b, seq, d)


def kernel(x, norm1_g, w_in, attn_q_norm_g, attn_k_norm_g, attn_sink, gla_wa2_fwd, gla_ba_fwd, gla_wa2_bwd, gla_ba_bwd, gla_out_norm_g, w_out, norm2_g, w_up, conv_w, conv_b, w_down):
    for l in range(w_in.shape[0]):
        x = _layer(x, norm1_g[l], w_in[l], attn_q_norm_g[l], attn_k_norm_g[l], attn_sink[l],
                   gla_wa2_fwd[l], gla_ba_fwd[l], gla_wa2_bwd[l], gla_ba_bwd[l], gla_out_norm_g[l],
                   w_out[l], norm2_g[l], w_up[l], conv_w[l], conv_b[l], w_down[l])
    return x
```

```python
import functools

import jax
import jax.numpy as jnp
import numpy as np
from jax import lax
from jax.experimental import pallas as pl
from jax.experimental.pallas import tpu as pltpu

F32 = jnp.float32
BF16 = jnp.bfloat16

HEAD_DIM = 128
KV_HEADS = 2
GQA_GROUP = 4
ATTN_HEADS = KV_HEADS * GQA_GROUP
ATTN_WIDTH = ATTN_HEADS * HEAD_DIM
WINDOW = 128
ROPE_THETA = 10000.0
GLA_HEADS = 4
GLA_DK = 128
GLA_DV = 256
GLA_KEY_WIDTH = GLA_HEADS * GLA_DK
GLA_WIDTH = GLA_HEADS * GLA_DV
GLA_DECAY_RANK = 16
GLA_GATE_NORMALIZER = 16.0
GLA_CHUNK = 64
GLA_BLOCK = 256
NORM_EPS = 1e-6

OFF_QA = 0
OFF_KA = OFF_QA + ATTN_WIDTH
OFF_VA = OFF_KA + KV_HEADS * HEAD_DIM
OFF_QG = OFF_VA + KV_HEADS * HEAD_DIM
OFF_KG = OFF_QG + GLA_KEY_WIDTH
OFF_VG = OFF_KG + GLA_KEY_WIDTH
OFF_GG = OFF_VG + GLA_WIDTH
OFF_LR = OFF_GG + GLA_WIDTH
MAIN_WIDTH = OFF_LR
LR_PAD = 128

VMEM_LIMIT_BYTES = 56 * 1024 * 1024
INPROJ_TM = 1024
INPROJ_TN = 1536
ATTN_TQ = 128
ROPE_ROWS = 512
OUTPROJ_TM = 512
FFN_TM = 1024
FFN_TF = 512
FFN_XROWS = 128
HALO = 16


def _cparams(*sem):
    return pltpu.CompilerParams(dimension_semantics=sem, vmem_limit_bytes=VMEM_LIMIT_BYTES)


def _inproj_kernel(x_ref, g_ref, w_ref, wlr_ref, wu_ref, wd_ref, wo_ref,
                   o_ref, lr_ref, wu_o_ref, wd_o_ref, wo_o_ref, h_ref):
    def cast_weight_blocks():
        wu_o_ref[...] = wu_ref[...].astype(BF16)
        wd_o_ref[...] = wd_ref[...].astype(BF16)
        wo_o_ref[...] = wo_ref[...].astype(BF16)

    @pl.when(pl.program_id(1) == 0)
    def _():
        x = x_ref[...]
        ms = jnp.mean(x * x, axis=-1, keepdims=True)
        h = (x * lax.rsqrt(ms + NORM_EPS) * g_ref[...]).astype(BF16)
        h_ref[...] = h
        o_ref[...] = jnp.dot(h, w_ref[...], preferred_element_type=F32).astype(o_ref.dtype)
        lr_ref[...] = jnp.dot(h, wlr_ref[...], preferred_element_type=F32)
        cast_weight_blocks()

    @pl.when(pl.program_id(1) > 0)
    def _():
        o_ref[...] = jnp.dot(h_ref[...], w_ref[...], preferred_element_type=F32).astype(o_ref.dtype)
        cast_weight_blocks()


def _cast_block_count(steps, extent, unit):
    return max(n for n in range(1, steps + 1) if extent % (n * unit) == 0)


def _inproj(x2, g1, w_main, w_lr, w_up, w_down, w_out):
    m, d = x2.shape
    tm, tn = INPROJ_TM, INPROJ_TN
    assert m % tm == 0 and MAIN_WIDTH % tn == 0 and w_main.shape == (d, MAIN_WIDTH + 2 * GLA_DECAY_RANK)
    nj = MAIN_WIDTH // tn
    steps = (m // tm) * nj
    nu = _cast_block_count(steps, w_up.shape[1], 128)
    nd = _cast_block_count(steps, w_down.shape[0], HALO)
    no = _cast_block_count(steps, w_out.shape[0], HALO)
    ub, db, ob = w_up.shape[1] // nu, w_down.shape[0] // nd, w_out.shape[0] // no

    def cast_block(n):
        return lambda i, j: jnp.minimum(i * nj + j, n - 1)

    return pl.pallas_call(
        _inproj_kernel,
        grid=(m // tm, nj),
        in_specs=[
            pl.BlockSpec((tm, d), lambda i, j: (i, 0)),
            pl.BlockSpec((1, d), lambda i, j: (0, 0)),
            pl.BlockSpec((d, tn), lambda i, j: (0, j)),
            pl.BlockSpec((d, LR_PAD), lambda i, j: (0, 0)),
            pl.BlockSpec((d, ub), lambda i, j: (0, cast_block(nu)(i, j))),
            pl.BlockSpec((db, d), lambda i, j: (cast_block(nd)(i, j), 0)),
            pl.BlockSpec((ob, d), lambda i, j: (cast_block(no)(i, j), 0)),
        ],
        out_specs=[
            pl.BlockSpec((tm, tn), lambda i, j: (i, j)),
            pl.BlockSpec((tm, LR_PAD), lambda i, j: (i, 0)),
            pl.BlockSpec((d, ub), lambda i, j: (0, cast_block(nu)(i, j))),
            pl.BlockSpec((db, d), lambda i, j: (cast_block(nd)(i, j), 0)),
            pl.BlockSpec((ob, d), lambda i, j: (cast_block(no)(i, j), 0)),
        ],
        out_shape=[
            jax.ShapeDtypeStruct((m, MAIN_WIDTH), BF16),
            jax.ShapeDtypeStruct((m, LR_PAD), F32),
            jax.ShapeDtypeStruct(w_up.shape, BF16),
            jax.ShapeDtypeStruct(w_down.shape, BF16),
            jax.ShapeDtypeStruct(w_out.shape, BF16),
        ],
        scratch_shapes=[pltpu.VMEM((tm, d), BF16)],
        compiler_params=_cparams("arbitrary", "arbitrary"),
        name="inproj",
    )(x2, g1, w_main, w_lr, w_up, w_down, w_out)


def _norm_rope(t, gain, cos, sin):
    t = t * lax.rsqrt(jnp.mean(t * t, axis=-1, keepdims=True) + NORM_EPS) * gain
    return t * cos + pltpu.roll(t, HEAD_DIM // 2, 1) * sin


def _attn_kernel(sink_ref, q_ref, k_ref, v_ref, cos_ref, sin_ref, qg_ref, kg_ref, o_ref, kr_ref,
                 *, tq, seq):
    i = pl.program_id(1)

    @pl.when(i == 0)
    def _():
        def body(c, carry):
            r0 = pl.multiple_of(c * ROPE_ROWS, ROPE_ROWS)
            cs = cos_ref[pl.ds(r0, ROPE_ROWS), :]
            sn = sin_ref[pl.ds(r0, ROPE_ROWS), :]
            for h in range(KV_HEADS):
                cols = slice(h * HEAD_DIM, (h + 1) * HEAD_DIM)
                kk = k_ref[0, pl.ds(r0, ROPE_ROWS), cols].astype(F32)
                kr_ref[pl.ds(r0, ROPE_ROWS), cols] = _norm_rope(kk, kg_ref[...], cs, sn).astype(BF16)
            return carry

        lax.fori_loop(0, seq // ROPE_ROWS, body, 0)

    win = tq + 2 * WINDOW
    q0 = pl.multiple_of(i * tq, tq)
    start = pl.multiple_of(jnp.clip(q0 - WINDOW, 0, seq - win), WINDOW)
    cs = cos_ref[pl.ds(q0, tq), :]
    sn = sin_ref[pl.ds(q0, tq), :]
    row = lax.broadcasted_iota(jnp.int32, (tq, win), 0)
    col = lax.broadcasted_iota(jnp.int32, (tq, win), 1)
    mask = (jnp.abs((start + col) - (q0 + row)) <= WINDOW)[None]
    grp = lax.broadcasted_iota(jnp.int32, (GQA_GROUP, 1, 1), 0)
    scale = HEAD_DIM ** -0.5

    for kvh in range(KV_HEADS):
        cols = slice(kvh * HEAD_DIM, (kvh + 1) * HEAD_DIM)
        kw = kr_ref[pl.ds(start, win), cols]
        vw = v_ref[0, pl.ds(start, win), cols]
        qs = []
        sink = jnp.zeros((GQA_GROUP, 1, 1), F32)
        for g in range(GQA_GROUP):
            h = kvh * GQA_GROUP + g
            qq = q_ref[0, :, h * HEAD_DIM:(h + 1) * HEAD_DIM].astype(F32)
            qs.append((_norm_rope(qq, qg_ref[...], cs, sn) * scale).astype(BF16))
            sink = jnp.where(grp == g, sink_ref[h], sink)
        qst = jnp.concatenate(qs, axis=0)
        s = lax.dot_general(qst, kw, (((1,), (1,)), ((), ())), preferred_element_type=F32)
        s = jnp.where(mask, s.reshape(GQA_GROUP, tq, win), -jnp.inf)
        m = jnp.maximum(jnp.max(s, axis=-1, keepdims=True), sink)
        p = jnp.exp(s - m)
        denom = jnp.sum(p, axis=-1, keepdims=True) + jnp.exp(sink - m)
        o = jnp.dot(p.astype(BF16).reshape(GQA_GROUP * tq, win), vw, preferred_element_type=F32)
        o = o.reshape(GQA_GROUP, tq, HEAD_DIM) / denom
        for g in range(GQA_GROUP):
            h = kvh * GQA_GROUP + g
            o_ref[0, :, h * HEAD_DIM:(h + 1) * HEAD_DIM] = o[g].astype(o_ref.dtype)


def _attention(proj3, cos, sin, qg, kg, sink):
    b, seq, _ = proj3.shape
    tq = ATTN_TQ
    kvw = KV_HEADS * HEAD_DIM
    assert seq % tq == 0 and seq % ROPE_ROWS == 0 and seq >= tq + 2 * WINDOW and tq % WINDOW == 0
    return pl.pallas_call(
        functools.partial(_attn_kernel, tq=tq, seq=seq),
        grid=(b, seq // tq),
        in_specs=[
            pl.BlockSpec(memory_space=pltpu.SMEM),
            pl.BlockSpec((1, tq, ATTN_WIDTH), lambda bi, i: (bi, i, OFF_QA // ATTN_WIDTH)),
            pl.BlockSpec((1, seq, kvw), lambda bi, i: (bi, 0, OFF_KA // kvw)),
            pl.BlockSpec((1, seq, kvw), lambda bi, i: (bi, 0, OFF_VA // kvw)),
            pl.BlockSpec((seq, HEAD_DIM), lambda bi, i: (0, 0)),
            pl.BlockSpec((seq, HEAD_DIM), lambda bi, i: (0, 0)),
            pl.BlockSpec((1, HEAD_DIM), lambda bi, i: (0, 0)),
            pl.BlockSpec((1, HEAD_DIM), lambda bi, i: (0, 0)),
        ],
        out_specs=pl.BlockSpec((1, tq, ATTN_WIDTH), lambda bi, i: (bi, i, 0)),
        out_shape=jax.ShapeDtypeStruct((b, seq, ATTN_WIDTH), BF16),
        scratch_shapes=[pltpu.VMEM((seq, kvw), BF16)],
        compiler_params=_cparams("parallel", "arbitrary"),
        name="swa",
    )(sink, proj3, proj3, proj3, cos, sin, qg, kg)


def _log_sigmoid(z):
    return jnp.minimum(z, 0.0) - jnp.log(1.0 + jnp.exp(-jnp.abs(z)))


def _split2(g):
    hi = g.astype(BF16)
    lo = (g - hi.astype(F32)).astype(BF16)
    return hi, lo


def _gla_kernel(q_ref, k_ref, v_ref, gate_ref, lr_ref, wa_ref, ba_ref, gn_ref,
                o_ref, qin_ref, kv_ref, dec_ref, acc_ref, st_ref, *, seq):
    c = GLA_CHUNK
    n = seq // c
    blk = GLA_BLOCK
    cpb = blk // c

    ri = lax.broadcasted_iota(jnp.int32, (blk, blk), 0)
    ci = lax.broadcasted_iota(jnp.int32, (blk, blk), 1)
    same = (ri // c) == (ci // c)
    lower_incl = same & (ci <= ri)
    upper_strict = same & (ci > ri)
    tri_f = jnp.where(lower_incl, 1.0, 0.0).astype(BF16)
    qscale = GLA_DK ** -0.5

    def intra(t, carry):
        r0 = pl.multiple_of(t * blk, blk)
        lr = lr_ref[0, pl.ds(r0, blk), :].astype(BF16)
        q3 = (q_ref[0, pl.ds(r0, blk), :].astype(F32) * qscale).reshape(cpb, c, GLA_DK)
        k3 = k_ref[0, pl.ds(r0, blk), :].astype(F32).reshape(cpb, c, GLA_DK)
        v = v_ref[0, pl.ds(r0, blk), :]

        z = jnp.dot(lr, wa_ref[...], preferred_element_type=F32) + ba_ref[...]
        g = _log_sigmoid(z) / GLA_GATE_NORMALIZER
        g_hi, g_lo = _split2(g)
        pre = jnp.dot(tri_f, jnp.concatenate([g_hi, g_lo], axis=1), preferred_element_type=F32)
        prefix = pre[:, :2 * GLA_DK] + pre[:, 2 * GLA_DK:]
        bcum_f = prefix[:, :GLA_DK].reshape(cpb, c, GLA_DK)
        pref_b = prefix[:, GLA_DK:].reshape(cpb, c, GLA_DK)
        bcum_b = pref_b[:, c - 1:c, :] - pref_b + g[:, GLA_DK:].reshape(cpb, c, GLA_DK)

        def one_direction(bcum, keep, ref_row, last_row, slot):
            b_mid = bcum[:, ref_row:ref_row + 1, :]
            b_end = bcum[:, last_row:last_row + 1, :]
            qe = q3 * jnp.exp(bcum - b_mid)
            ke = k3 * jnp.exp(b_mid - bcum)
            qin_ref[slot, pl.ds(r0, blk), :] = (qe * jnp.exp(b_mid)).astype(BF16).reshape(blk, GLA_DK)
            kout = (ke * jnp.exp(b_end - b_mid)).astype(BF16)
            for j in range(cpb):
                kv_ref[slot, t * cpb + j] = lax.dot_general(
                    v[j * c:(j + 1) * c], kout[j], (((0,), (0,)), ((), ())), preferred_element_type=F32)
            dec = jnp.broadcast_to(jnp.exp(b_end), (cpb, 8, GLA_DK)).reshape(cpb * 8, GLA_DK)
            dec_ref[slot, pl.ds(pl.multiple_of(t * (cpb * 8), cpb * 8), cpb * 8), :] = dec
            a = lax.dot_general(qe.astype(BF16).reshape(blk, GLA_DK), ke.astype(BF16).reshape(blk, GLA_DK),
                                (((1,), (1,)), ((), ())), preferred_element_type=F32)
            return jnp.where(keep, a, 0.0).astype(BF16)

        a_f = one_direction(bcum_f, lower_incl, c // 2, c - 1, 0)
        a_b = one_direction(bcum_b, upper_strict, c - 1 - c // 2, 0, 1)
        o = jnp.dot(jnp.concatenate([a_f, a_b], axis=1), jnp.concatenate([v, v], axis=0),
                    preferred_element_type=F32)
        acc_ref[pl.ds(r0, blk), :] = o
        return carry

    lax.fori_loop(0, seq // blk, intra, 0, unroll=8)

    def step(idx, slot):
        r0 = pl.multiple_of(idx * c, c)
        st = st_ref[slot]
        o = lax.dot_general(qin_ref[slot, pl.ds(r0, c), :], st.astype(BF16), (((1,), (1,)), ((), ())),
                            preferred_element_type=F32)
        dec = dec_ref[slot, pl.ds(pl.multiple_of(idx * 8, 8), 8), :]
        st_ref[slot] = ((st.reshape(GLA_DV // 8, 8, GLA_DK) * dec[None]).reshape(GLA_DV, GLA_DK)
                        + kv_ref[slot, idx])
        return r0, o

    def accumulate(r0, o):
        acc_ref[pl.ds(r0, c), :] += o

    def finalize(r0, o):
        o = o + acc_ref[pl.ds(r0, c), :]
        y = o * lax.rsqrt(jnp.mean(o * o, axis=-1, keepdims=True) + NORM_EPS) * gn_ref[...]
        gate = gate_ref[0, pl.ds(r0, c), :].astype(F32)
        o_ref[0, pl.ds(r0, c), :] = (y * (gate * jax.nn.sigmoid(gate))).astype(o_ref.dtype)

    st_ref[...] = jnp.zeros_like(st_ref)

    def first_half(t, carry):
        accumulate(*step(t, 0))
        accumulate(*step(n - 1 - t, 1))
        return carry

    def second_half(t, carry):
        finalize(*step(t, 0))
        finalize(*step(n - 1 - t, 1))
        return carry

    lax.fori_loop(0, n // 2, first_half, 0, unroll=8)
    lax.fori_loop(n // 2, n, second_half, 0, unroll=8)


def _gla(proj3, lr3, wa, ba, gn):
    b, seq, _ = proj3.shape
    assert (seq // GLA_CHUNK) % 8 == 0 and seq % GLA_BLOCK == 0
    return pl.pallas_call(
        functools.partial(_gla_kernel, seq=seq),
        grid=(b, GLA_HEADS),
        in_specs=[
            pl.BlockSpec((1, seq, GLA_DK), lambda bi, h: (bi, 0, OFF_QG // GLA_DK + h)),
            pl.BlockSpec((1, seq, GLA_DK), lambda bi, h: (bi, 0, OFF_KG // GLA_DK + h)),
            pl.BlockSpec((1, seq, GLA_DV), lambda bi, h: (bi, 0, OFF_VG // GLA_DV + h)),
            pl.BlockSpec((1, seq, GLA_DV), lambda bi, h: (bi, 0, OFF_GG // GLA_DV + h)),
            pl.BlockSpec((1, seq, LR_PAD), lambda bi, h: (bi, 0, 0)),
            pl.BlockSpec((LR_PAD, 2 * GLA_DK), lambda bi, h: (0, h)),
            pl.BlockSpec((1, 2 * GLA_DK), lambda bi, h: (0, h)),
            pl.BlockSpec((1, GLA_DV), lambda bi, h: (0, 0)),
        ],
        out_specs=pl.BlockSpec((1, seq, GLA_DV), lambda bi, h: (bi, 0, h)),
        out_shape=jax.ShapeDtypeStruct((b, seq, GLA_WIDTH), BF16),
        scratch_shapes=[
            pltpu.VMEM((2, seq, GLA_DK), BF16),
            pltpu.VMEM((2, seq // GLA_CHUNK, GLA_DV, GLA_DK), F32),
            pltpu.VMEM((2, seq // GLA_CHUNK * 8, GLA_DK), F32),
            pltpu.VMEM((seq, GLA_DV), F32),
            pltpu.VMEM((2, GLA_DV, GLA_DK), F32),
        ],
        compiler_params=_cparams("parallel", "arbitrary"),
        name="gla",
    )(proj3, proj3, proj3, proj3, lr3, wa, ba, gn)


def _outproj_kernel(oa_ref, og_ref, x_ref, w_ref, g_ref, x1_ref, h2_ref):
    wa = oa_ref.shape[1]
    acc = jnp.dot(oa_ref[...], w_ref[:wa, :], preferred_element_type=F32)
    acc = acc + jnp.dot(og_ref[...], w_ref[wa:, :], preferred_element_type=F32)
    x1 = x_ref[...] + acc
    x1_ref[...] = x1
    ms = jnp.mean(x1 * x1, axis=-1, keepdims=True)
    h2_ref[...] = (x1 * lax.rsqrt(ms + NORM_EPS) * g_ref[...]).astype(h2_ref.dtype)


def _outproj(oa, og, x2, w_out, g2):
    m, d = x2.shape
    tm = OUTPROJ_TM
    wa, wg = oa.shape[1], og.shape[1]
    assert m % tm == 0 and w_out.shape == (wa + wg, d)
    return pl.pallas_call(
        _outproj_kernel,
        grid=(m // tm,),
        in_specs=[
            pl.BlockSpec((tm, wa), lambda i: (i, 0)),
            pl.BlockSpec((tm, wg), lambda i: (i, 0)),
            pl.BlockSpec((tm, d), lambda i: (i, 0)),
            pl.BlockSpec((wa + wg, d), lambda i: (0, 0)),
            pl.BlockSpec((1, d), lambda i: (0, 0)),
        ],
        out_specs=[
            pl.BlockSpec((tm, d), lambda i: (i, 0)),
            pl.BlockSpec((tm, d), lambda i: (i, 0)),
        ],
        out_shape=[
            jax.ShapeDtypeStruct((m, d), F32),
            jax.ShapeDtypeStruct((m, d), BF16),
        ],
        compiler_params=_cparams("parallel"),
        name="outproj",
    )(oa, og, x2, w_out, g2)


def _ffn_kernel(h_ref, hp_ref, hn_ref, x1_ref, wg_ref, wv_ref, cwg_ref, cwv_ref, cbg_ref, cbv_ref,
                wd_ref, o_ref, hext_ref, *, tm, tiles_per_seq):
    i = pl.program_id(0)
    f = pl.program_id(1)

    th = tm // 2
    rows = th + HALO

    @pl.when(f == 0)
    def _():
        pos = i % tiles_per_seq
        r = lax.broadcasted_iota(jnp.int32, (HALO, 1), 0)
        zero = jnp.zeros_like(hn_ref[...])
        prev0 = jnp.where((r == HALO - 1) & (pos != 0), hp_ref[...], zero)
        next1 = jnp.where((r == 0) & (pos != tiles_per_seq - 1), hn_ref[...], zero)
        hext_ref[0, :th, :] = h_ref[:th, :]
        hext_ref[0, th:, :] = jnp.where(r == 0, h_ref[th:th + HALO, :], prev0)
        hext_ref[1, :th, :] = h_ref[th:, :]
        hext_ref[1, th:, :] = jnp.where(r == HALO - 1, h_ref[th - HALO:th, :], next1)
        o_ref[...] = jnp.zeros_like(o_ref)

    def conv(u, cw_ref, cb_ref):
        prev = pltpu.roll(u, 1, 0)
        nxt = pltpu.roll(u, rows - 1, 0)
        y = prev * cw_ref[0:1, :] + u * cw_ref[1:2, :] + nxt * cw_ref[2:3, :] + cb_ref[...]
        return y[:th]

    acts = []
    for half in range(2):
        hext = hext_ref[half]
        ug = conv(jnp.dot(hext, wg_ref[...], preferred_element_type=F32), cwg_ref, cbg_ref)
        uv = conv(jnp.dot(hext, wv_ref[...], preferred_element_type=F32), cwv_ref, cbv_ref)
        acts.append((ug * jax.nn.sigmoid(ug) * uv).astype(BF16))
    for half in range(2):
        o_ref[half * th:(half + 1) * th, :] += jnp.dot(acts[half], wd_ref[...], preferred_element_type=F32)

    @pl.when(f < tm // FFN_XROWS)
    def _():
        r0 = pl.multiple_of(f * FFN_XROWS, FFN_XROWS)
        o_ref[pl.ds(r0, FFN_XROWS), :] += x1_ref[...]


def _ffn(h2, x1, w_up, conv_w, conv_b, w_down, seq):
    m, d = h2.shape
    dff = w_down.shape[0]
    tm, tf = FFN_TM, FFN_TF
    nf = dff // tf
    hb = tm // HALO
    last_halo = m // HALO - 1
    xch = tm // FFN_XROWS
    assert nf >= xch and dff % tf == 0 and seq % tm == 0 and m % seq == 0 and w_up.shape == (d, 2 * dff)
    return pl.pallas_call(
        functools.partial(_ffn_kernel, tm=tm, tiles_per_seq=seq // tm),
        grid=(m // tm, nf),
        in_specs=[
            pl.BlockSpec((tm, d), lambda i, f: (i, 0)),
            pl.BlockSpec((HALO, d), lambda i, f: (jnp.maximum(i * hb - 1, 0), 0)),
            pl.BlockSpec((HALO, d), lambda i, f: (jnp.minimum((i + 1) * hb, last_halo), 0)),
            pl.BlockSpec((FFN_XROWS, d), lambda i, f: (i * xch + jnp.minimum(f, xch - 1), 0)),
            pl.BlockSpec((d, tf), lambda i, f: (0, f)),
            pl.BlockSpec((d, tf), lambda i, f: (0, nf + f)),
            pl.BlockSpec((3, tf), lambda i, f: (0, f)),
            pl.BlockSpec((3, tf), lambda i, f: (0, nf + f)),
            pl.BlockSpec((1, tf), lambda i, f: (0, f)),
            pl.BlockSpec((1, tf), lambda i, f: (0, nf + f)),
            pl.BlockSpec((tf, d), lambda i, f: (f, 0)),
        ],
        out_specs=pl.BlockSpec((tm, d), lambda i, f: (i, 0)),
        out_shape=jax.ShapeDtypeStruct((m, d), F32),
        scratch_shapes=[
            pltpu.VMEM((2, tm // 2 + HALO, d), BF16),
        ],
        compiler_params=_cparams("parallel", "arbitrary"),
        name="convffn",
    )(h2, h2, h2, x1, w_up, w_up, conv_w, conv_w, conv_b, conv_b, w_down)


def _rope_tables(seq):
    half = HEAD_DIM // 2
    inv = 1.0 / (ROPE_THETA ** (np.arange(half, dtype=np.float64) / half))
    ang = np.arange(seq, dtype=np.float64)[:, None] * inv[None, :]
    cos = np.cos(ang).astype(np.float32)
    sin = np.sin(ang).astype(np.float32)
    return (jnp.asarray(np.concatenate([cos, cos], axis=-1)),
            jnp.asarray(np.concatenate([-sin, sin], axis=-1)))


def _decay_params(wa2_f, ba_f, wa2_b, ba_b):
    r = GLA_DECAY_RANK
    wf = jnp.pad(wa2_f, ((0, LR_PAD - r), (0, 0))).reshape(LR_PAD, GLA_HEADS, 1, GLA_DK)
    wb = jnp.pad(wa2_b, ((r, LR_PAD - 2 * r), (0, 0))).reshape(LR_PAD, GLA_HEADS, 1, GLA_DK)
    wa = jnp.concatenate([wf, wb], axis=2).reshape(LR_PAD, 2 * GLA_KEY_WIDTH).astype(BF16)
    ba = jnp.concatenate([ba_f.reshape(GLA_HEADS, 1, GLA_DK), ba_b.reshape(GLA_HEADS, 1, GLA_DK)],
                         axis=1).reshape(1, 2 * GLA_KEY_WIDTH)
    return wa, ba


def _layer(x, norm1_g, w_in, q_norm_g, k_norm_g, sink, wa2_f, ba_f, wa2_b, ba_b, gla_norm_g,
           w_out, norm2_g, w_up, conv_w, conv_b, w_down):
    b, seq, d = x.shape
    m = b * seq
    x2 = x.reshape(m, d)
    w_lr = jnp.pad(w_in[:, MAIN_WIDTH:], ((0, 0), (0, LR_PAD - 2 * GLA_DECAY_RANK))).astype(BF16)
    proj, lr, w_up_b, w_down_b, w_out_b = _inproj(x2, norm1_g.reshape(1, d), w_in.astype(BF16), w_lr,
                                                  w_up, w_down, w_out)
    proj3 = proj.reshape(b, seq, MAIN_WIDTH)
    cos, sin = _rope_tables(seq)
    o_attn = _attention(proj3, cos, sin, q_norm_g.reshape(1, HEAD_DIM), k_norm_g.reshape(1, HEAD_DIM), sink)
    wa, ba = _decay_params(wa2_f, ba_f, wa2_b, ba_b)
    o_gla = _gla(proj3, lr.reshape(b, seq, LR_PAD), wa, ba, gla_norm_g.reshape(1, GLA_DV))
    x1, h2 = _outproj(o_attn.reshape(m, ATTN_WIDTH), o_gla.reshape(m, GLA_WIDTH), x2,
                      w_out_b, norm2_g.reshape(1, d))
    out = _ffn(h2, x1, w_up_b, conv_w, conv_b.reshape(1, -1), w_down_b, seq)
    return out.reshape(b, seq, d)


def kernel(x, norm1_g, w_in, attn_q_norm_g, attn_k_norm_g, attn_sink, gla_wa2_fwd, gla_ba_fwd, gla_wa2_bwd, gla_ba_bwd, gla_out_norm_g, w_out, norm2_g, w_up, conv_w, conv_b, w_down):
    for l in range(w_in.shape[0]):
        x = _layer(x, norm1_g[l], w_in[l], attn_q_norm_g[l], attn_k_norm_g[l], attn_sink[l],
                   gla_wa2_fwd[l], gla_ba_fwd[l], gla_wa2_bwd[l], gla_ba_bwd[l], gla_out_norm_g[l],
                   w_out[l], norm2_g[l], w_up[l], conv_w[l], conv_b[l], w_down[l])
    return x
```
